```python
import math
import jax, jax.numpy as jnp
from jax import lax
import numpy as np

D_MODEL = 1024
BATCH = 16
SEQ = 4096
DEPTH = 2

CHUNK = 64
N_MIXERS = 2
MIX_WIDTH = D_MODEL // 2
SGU_CHUNK = 128
SGU_HEADS = 4
SGU_HEAD_DIM = MIX_WIDTH // SGU_HEADS
CONV_WIDTH = 3
N_MEM = 256
XA_HEADS = 4
XA_HEAD_DIM = MIX_WIDTH // XA_HEADS
N_EXPERTS = 16
N_GROUPS = 4
GROUP_SIZE = N_EXPERTS // N_GROUPS
TOP_K = 2
D_EXPERT = D_MODEL // 2
N_A = (DEPTH + 1) // 2
N_B = DEPTH // 2
ALPHA = (2 * DEPTH) ** 0.25
BETA = (8 * DEPTH) ** -0.25
LN_EPS = 1e-5

kernel_name = "hybrid_gmlp_shortconv_memxattn_grouped_moe"


def layer_norm(x, g, b):
    xf = x.astype(jnp.float32)
    mu = xf.mean(-1, keepdims=True)
    var = jnp.square(xf - mu).mean(-1, keepdims=True)
    y = (xf - mu) * lax.rsqrt(var + LN_EPS) * g.astype(jnp.float32) + b.astype(jnp.float32)
    return y.astype(x.dtype)


def spatial_gating(z, ln_g, ln_b, w_s, b_s):
    u, v = jnp.split(z, 2, axis=-1)
    v = layer_norm(v, ln_g, ln_b)
    bsz, seq, _ = v.shape
    v = v.reshape(bsz, seq // SGU_CHUNK, SGU_CHUNK, SGU_HEADS, SGU_HEAD_DIM)
    frame_chunk = jnp.arange(SGU_CHUNK) // CHUNK
    mask = frame_chunk[:, None] >= frame_chunk[None, :]
    w = jnp.where(mask[None], w_s, jnp.zeros_like(w_s))
    mixed = jnp.einsum('hij,bcjhd->bcihd', w, v) + b_s.T[None, None, :, :, None]
    return u * mixed.reshape(bsz, seq, MIX_WIDTH)


def short_conv_mixer(z, conv_w):
    bg, cg, xt = jnp.split(z, 3, axis=-1)
    h = cg * xt
    h = lax.conv_general_dilated(
        h, conv_w[:, None, :].astype(h.dtype), window_strides=(1,),
        padding=[(CONV_WIDTH - 1, 0)], dimension_numbers=('NWC', 'WIO', 'NWC'),
        feature_group_count=MIX_WIDTH)
    return bg * h


def memory_cross_attention(q, mem, w_kv):
    bsz, seq, _ = q.shape
    n_mem = mem.shape[1]
    k, v = jnp.split(mem @ w_kv, 2, axis=-1)
    q = q.reshape(bsz, seq, XA_HEADS, XA_HEAD_DIM)
    k = k.reshape(bsz, n_mem, XA_HEADS, XA_HEAD_DIM)
    v = v.reshape(bsz, n_mem, XA_HEADS, XA_HEAD_DIM)
    s = jnp.einsum('bshd,bmhd->bhsm', q, k).astype(jnp.float32) * (XA_HEAD_DIM ** -0.5)
    p = jax.nn.softmax(s, axis=-1).astype(v.dtype)
    o = jnp.einsum('bhsm,bmhd->bshd', p, v)
    return o.reshape(bsz, seq, MIX_WIDTH)


def grouped_moe(x, router_w, router_b, w_gate, w_up, w_down):
    bsz, seq, d = x.shape
    xt = x.reshape(-1, d)
    scores = jax.nn.sigmoid(jnp.dot(xt.astype(jnp.float32), router_w.astype(jnp.float32)))
    sel = scores + router_b.astype(jnp.float32)
    group_score = lax.top_k(sel.reshape(-1, N_GROUPS, GROUP_SIZE), TOP_K)[0].sum(-1)
    top_group = jnp.argmax(group_score, axis=-1)
    expert_group = jnp.arange(N_EXPERTS) // GROUP_SIZE
    in_group = expert_group[None, :] == top_group[:, None]
    _, idx = lax.top_k(jnp.where(in_group, sel, -jnp.inf), TOP_K)
    gate = jnp.take_along_axis(scores, idx, axis=-1)
    gate = gate / gate.sum(-1, keepdims=True)
    combine = jnp.einsum('tk,tke->te', gate,
                         jax.nn.one_hot(idx, N_EXPERTS, dtype=jnp.float32)).astype(x.dtype)
    out = jnp.zeros_like(xt)
    for e in range(N_EXPERTS):
        h = jax.nn.silu(xt @ w_gate[e]) * (xt @ w_up[e])
        out = out + combine[:, e:e + 1] * (h @ w_down[e])
    return out.reshape(bsz, seq, d)


def setup_inputs(seed: int = 0) -> dict:
    key = jax.random.key(seed)
    ks = jax.random.split(key, 20)
    d = D_MODEL

    def nrm(k, shape, scale):
        return jax.random.normal(k, shape, jnp.float32) * scale

    x = nrm(ks[0], (BATCH, SEQ, d), 1.0)
    mem = nrm(ks[1], (BATCH, N_MEM, d), 1.0)
    w_in_a = nrm(ks[2], (N_A, d, 3 * MIX_WIDTH), d ** -0.5)
    sgu_ln_g = 1.0 + nrm(ks[3], (N_A, MIX_WIDTH), 0.02)
    sgu_ln_b = nrm(ks[4], (N_A, MIX_WIDTH), 0.02)
    sgu_w = nrm(ks[5], (N_A, SGU_HEADS, SGU_CHUNK, SGU_CHUNK), SGU_CHUNK ** -0.5)
    sgu_b = 1.0 + nrm(ks[6], (N_A, SGU_HEADS, SGU_CHUNK), 0.02)
    w_in_b = nrm(ks[7], (N_B, d, 4 * MIX_WIDTH), d ** -0.5)
    conv_w = nrm(ks[8], (N_B, CONV_WIDTH, MIX_WIDTH), CONV_WIDTH ** -0.5)
    kv_scale = jnp.concatenate([jnp.ones((MIX_WIDTH,), jnp.float32),
                                jnp.full((MIX_WIDTH,), BETA, jnp.float32)]) * (d ** -0.5)
    w_kv = nrm(ks[9], (DEPTH, d, 2 * MIX_WIDTH), 1.0) * kv_scale
    w_out = nrm(ks[10], (DEPTH, 2 * MIX_WIDTH, d), BETA * (2 * MIX_WIDTH) ** -0.5)
    ln1_g = 1.0 + nrm(ks[11], (DEPTH, d), 0.02)
    ln1_b = nrm(ks[12], (DEPTH, d), 0.02)
    router_w = nrm(ks[13], (d, N_EXPERTS), d ** -0.5)
    router_b = nrm(ks[14], (N_EXPERTS,), 0.01)
    w_gate = nrm(ks[15], (DEPTH, N_EXPERTS, d, D_EXPERT), d ** -0.5)
    w_up = nrm(ks[16], (DEPTH, N_EXPERTS, d, D_EXPERT), d ** -0.5)
    w_down = nrm(ks[17], (DEPTH, N_EXPERTS, D_EXPERT, d), BETA * D_EXPERT ** -0.5)
    ln2_g = 1.0 + nrm(ks[18], (DEPTH, d), 0.02)
    ln2_b = nrm(ks[19], (DEPTH, d), 0.02)
    return {"x": x, "mem": mem, "w_in_a": w_in_a, "sgu_ln_g": sgu_ln_g, "sgu_ln_b": sgu_ln_b,
            "sgu_w": sgu_w, "sgu_b": sgu_b, "w_in_b": w_in_b, "conv_w": conv_w,
            "w_kv": w_kv, "w_out": w_out, "ln1_g": ln1_g, "ln1_b": ln1_b,
            "router_w": router_w, "router_b": router_b, "w_gate": w_gate, "w_up": w_up,
            "w_down": w_down, "ln2_g": ln2_g, "ln2_b": ln2_b}


def reference(x, mem, w_in_a, sgu_ln_g, sgu_ln_b, sgu_w, sgu_b, w_in_b, conv_w,
              w_kv, w_out, ln1_g, ln1_b, router_w, router_b, w_gate, w_up, w_down,
              ln2_g, ln2_b):
    for i in range(DEPTH):
        j = i // N_MIXERS
        if i % N_MIXERS == 0:
            z = x @ w_in_a[j]
            uv = jax.nn.gelu(z[..., :2 * MIX_WIDTH], approximate=False)
            tok = spatial_gating(uv, sgu_ln_g[j], sgu_ln_b[j], sgu_w[j], sgu_b[j])
            q = z[..., 2 * MIX_WIDTH:]
        else:
            z = x @ w_in_b[j]
            tok = short_conv_mixer(z[..., :3 * MIX_WIDTH], conv_w[j])
            q = z[..., 3 * MIX_WIDTH:]
        xa = memory_cross_attention(q, mem, w_kv[i])
        o = jnp.concatenate([tok, xa], axis=-1) @ w_out[i]
        x = layer_norm(ALPHA * x + o, ln1_g[i], ln1_b[i])
        f = grouped_moe(x, router_w, router_b, w_gate[i], w_up[i], w_down[i])
        x = layer_norm(ALPHA * x + f, ln2_g[i], ln2_b[i])
    return x
```

```python
import functools
import math

import jax
import jax.numpy as jnp
from jax import lax
from jax.experimental import pallas as pl
from jax.experimental.pallas import tpu as pltpu

D_MODEL = 1024
BATCH = 16
SEQ = 4096
DEPTH = 2
CHUNK = 64
MIX_WIDTH = D_MODEL // 2
SGU_CHUNK = 128
SGU_HEADS = 4
SGU_HEAD_DIM = MIX_WIDTH // SGU_HEADS
CONV_WIDTH = 3
N_MEM = 256
XA_HEADS = 4
XA_HEAD_DIM = MIX_WIDTH // XA_HEADS
N_EXPERTS = 16
N_GROUPS = 4
GROUP_SIZE = N_EXPERTS // N_GROUPS
D_EXPERT = D_MODEL // 2
ALPHA = (2 * DEPTH) ** 0.25
LN_EPS = 1e-5

N_TOKENS = BATCH * SEQ
N_PAIRS = GROUP_SIZE * (GROUP_SIZE - 1) // 2
N_CLASSES = N_GROUPS * N_PAIRS
CLASS_ROWS = 32
PAIR_LO = (0, 0, 0, 1, 1, 2)
PAIR_HI = (1, 2, 3, 2, 3, 3)

TM = 512
MOE_TILE = 256
N_MOE_TILES = N_TOKENS // MOE_TILE + N_CLASSES
N_SORTED = N_MOE_TILES * MOE_TILE
HALO = 8
VMEM_LIMIT = 56 * 1024 * 1024

F32 = jnp.float32
BF16 = jnp.bfloat16


def _dot(a, b):
    return jnp.dot(a, b, preferred_element_type=F32)


def _dot_nt(a, b):
    return lax.dot_general(a, b, (((1,), (1,)), ((), ())), preferred_element_type=F32)


def _layer_norm(x, g, b):
    mu = jnp.mean(x, axis=-1, keepdims=True)
    xc = x - mu
    var = jnp.mean(xc * xc, axis=-1, keepdims=True)
    return xc * lax.rsqrt(var + LN_EPS) * g + b


def _sigmoid(x):
    return 1.0 / (1.0 + jnp.exp(-x))


def _kv_kernel(mem_ref, wkv_ref, kt_ref, v_ref):
    kv = _dot(mem_ref[0].astype(BF16), wkv_ref[0])
    kt_ref[0, 0] = kv[:, :MIX_WIDTH].T.astype(BF16)
    v_ref[0, 0] = kv[:, MIX_WIDTH:].astype(BF16)


def _kv_call(mem, wkv_bf16):
    return pl.pallas_call(
        _kv_kernel,
        grid=(DEPTH, BATCH),
        in_specs=[
            pl.BlockSpec((1, N_MEM, D_MODEL), lambda l, b: (b, 0, 0)),
            pl.BlockSpec((1, D_MODEL, 2 * MIX_WIDTH), lambda l, b: (l, 0, 0)),
        ],
        out_specs=[
            pl.BlockSpec((1, 1, MIX_WIDTH, N_MEM), lambda l, b: (l, b, 0, 0)),
            pl.BlockSpec((1, 1, N_MEM, MIX_WIDTH), lambda l, b: (l, b, 0, 0)),
        ],
        out_shape=[
            jax.ShapeDtypeStruct((DEPTH, BATCH, MIX_WIDTH, N_MEM), BF16),
            jax.ShapeDtypeStruct((DEPTH, BATCH, N_MEM, MIX_WIDTH), BF16),
        ],
        name="kv_proj",
    )(mem, wkv_bf16)


def _cross_attention(q, kt_ref, v_ref):
    outs = []
    scale = XA_HEAD_DIM ** -0.5
    for h in range(XA_HEADS):
        lo, hi = h * XA_HEAD_DIM, (h + 1) * XA_HEAD_DIM
        s = _dot(q[:, lo:hi].astype(BF16), kt_ref[0, 0, lo:hi, :]) * scale
        m = jnp.max(s, axis=-1, keepdims=True)
        e = jnp.exp(s - m)
        p = e * (1.0 / jnp.sum(e, axis=-1, keepdims=True))
        outs.append(_dot(p.astype(BF16), v_ref[0, 0, :, lo:hi]))
    return outs


def _route(y, rw_ref, rb_ref, tri_ref, carry_ref, meta_ref, cnt_ref):
    y_hi = y.astype(BF16)
    y_lo = (y - y_hi.astype(F32)).astype(BF16)
    l_hi = _dot_nt(rw_ref[...], y_hi)
    l_lo = _dot_nt(rw_ref[0:32, :], y_lo)
    logits = l_hi[0:32] + l_hi[32:64] + l_lo
    scores = _sigmoid(logits)
    sel = scores + rb_ref[...]

    row = lax.broadcasted_iota(jnp.int32, (8, TM), 0).astype(F32)
    s = [sel[8 * m:8 * m + 8] for m in range(GROUP_SIZE)]
    neg = jnp.float32(-jnp.inf)
    m1 = jnp.maximum(jnp.maximum(s[0], s[1]), jnp.maximum(s[2], s[3]))
    i1 = jnp.where(s[0] == m1, 0.0, jnp.where(s[1] == m1, 1.0, jnp.where(s[2] == m1, 2.0, 3.0)))
    t = [jnp.where(i1 == float(m), neg, s[m]) for m in range(GROUP_SIZE)]
    m2 = jnp.maximum(jnp.maximum(t[0], t[1]), jnp.maximum(t[2], t[3]))
    i2 = jnp.where(t[0] == m2, 0.0, jnp.where(t[1] == m2, 1.0, jnp.where(t[2] == m2, 2.0, 3.0)))
    gscore = jnp.where(row < float(N_GROUPS), m1 + m2, neg)
    gmax = jnp.max(gscore, axis=0, keepdims=True)
    top_group = jnp.min(jnp.where(gscore == gmax, row, 8.0), axis=0, keepdims=True)
    lo = jnp.minimum(i1, i2)
    hi = jnp.maximum(i1, i2)
    pair = jnp.where(lo == 0.0, 0.0, jnp.where(lo == 1.0, 3.0, 5.0)) + hi - lo - 1.0
    cls_all = row * float(N_PAIRS) + pair
    cls = jnp.sum(jnp.where(row == top_group, cls_all, 0.0), axis=0, keepdims=True)

    crow = lax.broadcasted_iota(jnp.int32, (CLASS_ROWS, TM), 0).astype(F32)
    onehot = jnp.where(crow == cls, 1.0, 0.0)
    before = _dot(onehot.astype(BF16), tri_ref[...])
    carry = carry_ref[...]
    rank = jnp.sum(onehot * (before + carry), axis=0, keepdims=True)
    carry_new = carry + jnp.sum(onehot, axis=1, keepdims=True)
    carry_ref[...] = carry_new
    cnt_ref[...] = carry_new[:, 0:128].astype(jnp.int32)

    mrow = lax.broadcasted_iota(jnp.int32, (8, TM), 0)
    meta = jnp.where(mrow == 0, cls, jnp.where(mrow == 1, rank, 0.0))
    meta_ref[0] = meta.astype(jnp.int32)


def _init_route_state(tri_ref, carry_ref):
    @pl.when(pl.program_id(0) == 0)
    def _():
        r = lax.broadcasted_iota(jnp.int32, (TM, TM), 0)
        c = lax.broadcasted_iota(jnp.int32, (TM, TM), 1)
        tri_ref[...] = (r < c).astype(BF16)
        carry_ref[...] = jnp.zeros_like(carry_ref)


def _finish_layer(x, tok, q, kt_ref, v_ref, wout_ref, g1_ref, b1_ref, rw_ref, rb_ref,
                  tri_ref, carry_ref, x1_ref, meta_ref, cnt_ref):
    heads = _cross_attention(q, kt_ref, v_ref)
    cat = jnp.concatenate([tok] + heads, axis=1).astype(BF16)
    o = _dot(cat, wout_ref[0])
    y = _layer_norm(ALPHA * x + o, g1_ref[0], b1_ref[0])
    x1_ref[...] = y
    _route(y, rw_ref, rb_ref, tri_ref, carry_ref, meta_ref, cnt_ref)


def _mixer_a_kernel(x_ref, win_ref, lng_ref, lnb_ref, sw_ref, sb_ref, kt_ref, v_ref, wout_ref,
                    g1_ref, b1_ref, rw_ref, rb_ref, x1_ref, meta_ref, cnt_ref, tri_ref, carry_ref):
    _init_route_state(tri_ref, carry_ref)
    x = x_ref[...]
    z = _dot(x.astype(BF16), win_ref[0])
    zuv = z[:, :2 * MIX_WIDTH]
    uv = zuv * (lax.erf(zuv * (2.0 ** -0.5)) + 1.0) * 0.5
    u = uv[:, :MIX_WIDTH]
    vn = _layer_norm(uv[:, MIX_WIDTH:], lng_ref[0], lnb_ref[0]).astype(BF16)
    pos_r = lax.broadcasted_iota(jnp.int32, (SGU_CHUNK, SGU_CHUNK), 0) // CHUNK
    pos_c = lax.broadcasted_iota(jnp.int32, (SGU_CHUNK, SGU_CHUNK), 1) // CHUNK
    causal = pos_r >= pos_c
    rows = []
    for c in range(TM // SGU_CHUNK):
        blocks = []
        for h in range(SGU_HEADS):
            w = jnp.where(causal, sw_ref[0, h], 0.0).astype(BF16)
            blk = vn[c * SGU_CHUNK:(c + 1) * SGU_CHUNK, h * SGU_HEAD_DIM:(h + 1) * SGU_HEAD_DIM]
            blocks.append(_dot(w, blk) + sb_ref[0, h])
        rows.append(jnp.concatenate(blocks, axis=1))
    mixed = jnp.concatenate(rows, axis=0)
    tok = u * mixed
    q = z[:, 2 * MIX_WIDTH:]
    _finish_layer(x, tok, q, kt_ref, v_ref, wout_ref, g1_ref, b1_ref, rw_ref, rb_ref,
                  tri_ref, carry_ref, x1_ref, meta_ref, cnt_ref)


def _mixer_b_kernel(x_ref, xh_ref, win_ref, cw_ref, kt_ref, v_ref, wout_ref,
                    g1_ref, b1_ref, rw_ref, rb_ref, x1_ref, meta_ref, cnt_ref, tri_ref, carry_ref):
    _init_route_state(tri_ref, carry_ref)
    x = x_ref[...]
    z = _dot(x.astype(BF16), win_ref[0])
    bg = z[:, :MIX_WIDTH]
    hcur = z[:, MIX_WIDTH:2 * MIX_WIDTH] * z[:, 2 * MIX_WIDTH:3 * MIX_WIDTH]
    zh = _dot(xh_ref[...].astype(BF16), win_ref[0, :, MIX_WIDTH:3 * MIX_WIDTH])
    first_of_seq = (pl.program_id(0) % (SEQ // TM)) == 0
    hprev = jnp.where(first_of_seq, 0.0, zh[:, :MIX_WIDTH] * zh[:, MIX_WIDTH:])
    rowid = lax.broadcasted_iota(jnp.int32, (TM, MIX_WIDTH), 0)
    h1 = jnp.where(rowid == 0, hprev[HALO - 1:HALO], pltpu.roll(hcur, 1, 0))
    h2 = jnp.where(rowid == 0, hprev[HALO - 2:HALO - 1],
                   jnp.where(rowid == 1, hprev[HALO - 1:HALO], pltpu.roll(hcur, 2, 0)))
    cw = cw_ref[0]
    conv = cw[0:1] * h2 + cw[1:2] * h1 + cw[2:3] * hcur
    tok = bg * conv
    q = z[:, 3 * MIX_WIDTH:]
    _finish_layer(x, tok, q, kt_ref, v_ref, wout_ref, g1_ref, b1_ref, rw_ref, rb_ref,
                  tri_ref, carry_ref, x1_ref, meta_ref, cnt_ref)


def _const_spec(shape):
    return pl.BlockSpec(shape, lambda i: (0,) * len(shape))


def _layer_spec(shape, layer):
    return pl.BlockSpec((1,) + shape, lambda i: (layer,) + (0,) * len(shape))


_TILES_PER_SEQ = SEQ // TM
_MIXER_OUT_SPECS = [
    pl.BlockSpec((TM, D_MODEL), lambda i: (i, 0)),
    pl.BlockSpec((1, 8, TM), lambda i: (i, 0, 0)),
    pl.BlockSpec((CLASS_ROWS, 128), lambda i: (0, 0)),
]
_MIXER_OUT_SHAPE = [
    jax.ShapeDtypeStruct((N_TOKENS, D_MODEL), F32),
    jax.ShapeDtypeStruct((N_TOKENS // TM, 8, TM), jnp.int32),
    jax.ShapeDtypeStruct((CLASS_ROWS, 128), jnp.int32),
]
_MIXER_SCRATCH = [pltpu.VMEM((TM, TM), BF16), pltpu.VMEM((CLASS_ROWS, TM), F32)]
_MIXER_PARAMS = pltpu.CompilerParams(dimension_semantics=("arbitrary",), vmem_limit_bytes=VMEM_LIMIT)


def _tail_specs(layer):
    return [
        pl.BlockSpec((1, 1, MIX_WIDTH, N_MEM), lambda i: (layer, i // _TILES_PER_SEQ, 0, 0)),
        pl.BlockSpec((1, 1, N_MEM, MIX_WIDTH), lambda i: (layer, i // _TILES_PER_SEQ, 0, 0)),
        _layer_spec((D_MODEL, D_MODEL), layer),
        _layer_spec((1, D_MODEL), layer),
        _layer_spec((1, D_MODEL), layer),
        _const_spec((64, D_MODEL)),
        _const_spec((32, 1)),
    ]


def _mixer_a_call(x, p, layer, j):
    return pl.pallas_call(
        _mixer_a_kernel,
        grid=(N_TOKENS // TM,),
        in_specs=[
            pl.BlockSpec((TM, D_MODEL), lambda i: (i, 0)),
            _layer_spec((D_MODEL, 3 * MIX_WIDTH), j),
            _layer_spec((1, MIX_WIDTH), j),
            _layer_spec((1, MIX_WIDTH), j),
            _layer_spec((SGU_HEADS, SGU_CHUNK, SGU_CHUNK), j),
            _layer_spec((SGU_HEADS, SGU_CHUNK, SGU_HEAD_DIM), j),
        ] + _tail_specs(layer),
        out_specs=_MIXER_OUT_SPECS,
        out_shape=_MIXER_OUT_SHAPE,
        scratch_shapes=_MIXER_SCRATCH,
        compiler_params=_MIXER_PARAMS,
        name="mixer_gmlp",
    )(x, p["w_in_a"], p["sgu_ln_g"], p["sgu_ln_b"], p["sgu_w"], p["sgu_b_col"],
      p["kt"], p["v"], p["w_out"], p["ln1_g"], p["ln1_b"], p["rw"], p["rb"])


def _mixer_b_call(x, p, layer, j):
    def halo_map(i):
        return (jnp.maximum(i * (TM // HALO) - 1, 0), 0)

    return pl.pallas_call(
        _mixer_b_kernel,
        grid=(N_TOKENS // TM,),
        in_specs=[
            pl.BlockSpec((TM, D_MODEL), lambda i: (i, 0)),
            pl.BlockSpec((HALO, D_MODEL), halo_map),
            _layer_spec((D_MODEL, 4 * MIX_WIDTH), j),
            _layer_spec((CONV_WIDTH, MIX_WIDTH), j),
        ] + _tail_specs(layer),
        out_specs=_MIXER_OUT_SPECS,
        out_shape=_MIXER_OUT_SHAPE,
        scratch_shapes=_MIXER_SCRATCH,
        compiler_params=_MIXER_PARAMS,
        name="mixer_conv",
    )(x, x, p["w_in_b"], p["conv_w"],
      p["kt"], p["v"], p["w_out"], p["ln1_g"], p["ln1_b"], p["rw"], p["rb"])


def _row_copy(src_ref, src_row, dst_ref, dst_row, sem):
    return pltpu.make_async_copy(src_ref.at[pl.ds(src_row, 1)], dst_ref.at[pl.ds(dst_row, 1)], sem)


def _dispatch_kernel(off_ref, pad_ref, nused_ref, meta_ref, x_ref, xs_ref, zero_ref, sem, zsem):
    @pl.when(pl.program_id(0) == 0)
    def _():
        zero_ref[...] = jnp.zeros_like(zero_ref)

        def clear_tile(start):
            cp = pltpu.make_async_copy(zero_ref, xs_ref.at[pl.ds(start, MOE_TILE)], zsem)
            cp.start()
            cp.wait()

        def clear_partial(c, carry):
            @pl.when(pad_ref[c] >= 0)
            def _():
                clear_tile(pl.multiple_of(pad_ref[c], MOE_TILE))
            return carry

        def clear_unused(t, carry):
            clear_tile(pl.multiple_of(t * MOE_TILE, MOE_TILE))
            return carry

        lax.fori_loop(0, N_CLASSES, clear_partial, 0)
        lax.fori_loop(nused_ref[0], N_MOE_TILES, clear_unused, 0)

    def issue(r, carry):
        dst = off_ref[meta_ref[0, 0, r]] + meta_ref[0, 1, r]
        _row_copy(x_ref, r, xs_ref, dst, sem).start()
        return carry

    lax.fori_loop(0, TM, issue, 0, unroll=8)
    pltpu.make_async_copy(x_ref, xs_ref.at[pl.ds(0, TM)], sem).wait()


def _dispatch_call(off, pad_start, n_used, meta, x1):
    return pl.pallas_call(
        _dispatch_kernel,
        grid_spec=pltpu.PrefetchScalarGridSpec(
            num_scalar_prefetch=3,
            grid=(N_TOKENS // TM,),
            in_specs=[
                pl.BlockSpec((1, 8, TM), lambda i, *_: (i, 0, 0), memory_space=pltpu.SMEM),
                pl.BlockSpec((TM, D_MODEL), lambda i, *_: (i, 0)),
            ],
            out_specs=pl.BlockSpec(memory_space=pl.ANY),
            scratch_shapes=[pltpu.VMEM((MOE_TILE, D_MODEL), F32),
                            pltpu.SemaphoreType.DMA(()), pltpu.SemaphoreType.DMA(())],
        ),
        out_shape=jax.ShapeDtypeStruct((N_SORTED, D_MODEL), F32),
        compiler_params=pltpu.CompilerParams(dimension_semantics=("arbitrary",)),
        name="moe_dispatch",
    )(off, pad_start, n_used, meta, x1)


def _gather_kernel(off_ref, meta_ref, ys_ref, out_ref, sem):
    def issue(r, carry):
        src = off_ref[meta_ref[0, 0, r]] + meta_ref[0, 1, r]
        _row_copy(ys_ref, src, out_ref, r, sem).start()
        return carry

    lax.fori_loop(0, TM, issue, 0, unroll=8)
    pltpu.make_async_copy(ys_ref.at[pl.ds(0, TM)], out_ref, sem).wait()


def _gather_call(off, meta, ys):
    return pl.pallas_call(
        _gather_kernel,
        grid_spec=pltpu.PrefetchScalarGridSpec(
            num_scalar_prefetch=1,
            grid=(N_TOKENS // TM,),
            in_specs=[
                pl.BlockSpec((1, 8, TM), lambda i, *_: (i, 0, 0), memory_space=pltpu.SMEM),
                pl.BlockSpec(memory_space=pl.ANY),
            ],
            out_specs=pl.BlockSpec((TM, D_MODEL), lambda i, *_: (i, 0)),
            scratch_shapes=[pltpu.SemaphoreType.DMA(())],
        ),
        out_shape=jax.ShapeDtypeStruct((N_TOKENS, D_MODEL), F32),
        compiler_params=pltpu.CompilerParams(dimension_semantics=("arbitrary",)),
        name="moe_gather",
    )(off, meta, ys)


def _moe_kernel(blk_ref, ea_ref, eb_ref, nused_ref, xs_ref, rw_ref, wgu_a_ref, wd_a_ref,
                wgu_b_ref, wd_b_ref, g2_ref, b2_ref, ys_ref):
    i = pl.program_id(0)

    @pl.when(i < nused_ref[0])
    def _():
        x = xs_ref[...]
        xb = x.astype(BF16)
        scores = _sigmoid(_dot(xb, rw_ref[...]))
        lane = lax.broadcasted_iota(jnp.int32, scores.shape, 1)
        sa = jnp.sum(jnp.where(lane == ea_ref[i], scores, 0.0), axis=-1, keepdims=True)
        sb = jnp.sum(jnp.where(lane == eb_ref[i], scores, 0.0), axis=-1, keepdims=True)
        inv = 1.0 / (sa + sb)

        def expert(wgu_ref, wd_ref):
            gu = _dot(xb, wgu_ref[0, 0])
            g = gu[:, :D_EXPERT]
            h = g * _sigmoid(g) * gu[:, D_EXPERT:]
            return _dot(h.astype(BF16), wd_ref[0, 0])

        f = (sa * inv) * expert(wgu_a_ref, wd_a_ref) + (sb * inv) * expert(wgu_b_ref, wd_b_ref)
        ys_ref[...] = _layer_norm(ALPHA * x + f, g2_ref[0], b2_ref[0])

    @pl.when(i >= nused_ref[0])
    def _():
        ys_ref[...] = jnp.zeros_like(ys_ref)


def _moe_call(tile_blk, tile_ea, tile_eb, n_used, xs, p, layer):
    def xmap(i, blk, ea, eb, nu):
        return (blk[i], 0)

    def wa(i, blk, ea, eb, nu):
        return (layer, ea[i], 0, 0)

    def wb(i, blk, ea, eb, nu):
        return (layer, eb[i], 0, 0)

    def lmap(i, *_):
        return (layer, 0, 0)

    return pl.pallas_call(
        _moe_kernel,
        grid_spec=pltpu.PrefetchScalarGridSpec(
            num_scalar_prefetch=4,
            grid=(N_MOE_TILES,),
            in_specs=[
                pl.BlockSpec((MOE_TILE, D_MODEL), xmap),
                pl.BlockSpec((D_MODEL, 128), lambda i, *_: (0, 0)),
                pl.BlockSpec((1, 1, D_MODEL, 2 * D_EXPERT), wa),
                pl.BlockSpec((1, 1, D_EXPERT, D_MODEL), wa),
                pl.BlockSpec((1, 1, D_MODEL, 2 * D_EXPERT), wb),
                pl.BlockSpec((1, 1, D_EXPERT, D_MODEL), wb),
                pl.BlockSpec((1, 1, D_MODEL), lmap),
                pl.BlockSpec((1, 1, D_MODEL), lmap),
            ],
            out_specs=pl.BlockSpec((MOE_TILE, D_MODEL), lambda i, *_: (i, 0)),
        ),
        out_shape=jax.ShapeDtypeStruct((N_SORTED, D_MODEL), F32),
        compiler_params=pltpu.CompilerParams(dimension_semantics=("arbitrary",),
                                             vmem_limit_bytes=VMEM_LIMIT),
        name="moe_experts",
    )(tile_blk, tile_ea, tile_eb, n_used, xs, p["rw_cols"], p["w_gu"], p["w_down"],
      p["w_gu"], p["w_down"], p["ln2_g"], p["ln2_b"])


def _tile_plan(counts):
    n_tiles = (counts + (MOE_TILE - 1)) // MOE_TILE
    ends = jnp.cumsum(n_tiles)
    starts = ends - n_tiles
    off = jnp.zeros((CLASS_ROWS,), jnp.int32).at[:N_CLASSES].set(starts * MOE_TILE)
    partial = (counts % MOE_TILE) != 0
    pad_start = jnp.where(partial, (ends - 1) * MOE_TILE, -1).astype(jnp.int32)
    n_used = ends[-1]
    tile_id = jnp.minimum(jnp.arange(N_MOE_TILES, dtype=jnp.int32), n_used - 1)
    tile_cls = jnp.sum(tile_id[:, None] >= ends[None, :], axis=1).astype(jnp.int32)
    pair = tile_cls % N_PAIRS
    base = (tile_cls // N_PAIRS) * GROUP_SIZE
    ea = base + jnp.asarray(PAIR_LO, jnp.int32)[pair]
    eb = base + jnp.asarray(PAIR_HI, jnp.int32)[pair]
    return off, pad_start, tile_id, ea, eb, n_used.reshape(1).astype(jnp.int32)


def _router_rows(router_w):
    wt = router_w.astype(F32).T.reshape(N_GROUPS, GROUP_SIZE, D_MODEL)
    wt = jnp.transpose(wt, (1, 0, 2))
    wt = jnp.pad(wt, ((0, 0), (0, 8 - N_GROUPS), (0, 0))).reshape(32, D_MODEL)
    hi = wt.astype(BF16)
    lo = (wt - hi.astype(F32)).astype(BF16)
    return jnp.concatenate([hi, lo], axis=0)


def _router_bias_rows(router_b):
    b = jnp.transpose(router_b.astype(F32).reshape(N_GROUPS, GROUP_SIZE), (1, 0))
    return jnp.pad(b, ((0, 0), (0, 8 - N_GROUPS))).reshape(32, 1)


def kernel(x, mem, w_in_a, sgu_ln_g, sgu_ln_b, sgu_w, sgu_b, w_in_b, conv_w, w_kv, w_out, ln1_g, ln1_b,
           router_w, router_b, w_gate, w_up, w_down, ln2_g, ln2_b):
    kt, v = _kv_call(mem, w_kv.astype(BF16))
    p = {
        "w_in_a": w_in_a.astype(BF16),
        "sgu_ln_g": sgu_ln_g[:, None, :],
        "sgu_ln_b": sgu_ln_b[:, None, :],
        "sgu_w": sgu_w,
        "sgu_b_col": jnp.broadcast_to(sgu_b[..., None], sgu_b.shape + (SGU_HEAD_DIM,)),
        "w_in_b": w_in_b.astype(BF16),
        "conv_w": conv_w,
        "kt": kt,
        "v": v,
        "w_out": w_out.astype(BF16),
        "ln1_g": ln1_g[:, None, :],
        "ln1_b": ln1_b[:, None, :],
        "rw": _router_rows(router_w),
        "rb": _router_bias_rows(router_b),
        "rw_cols": jnp.pad(router_w.astype(BF16), ((0, 0), (0, 128 - N_EXPERTS))),
        "w_gu": jnp.concatenate([w_gate, w_up], axis=-1).astype(BF16),
        "w_down": w_down.astype(BF16),
        "ln2_g": ln2_g[:, None, :],
        "ln2_b": ln2_b[:, None, :],
    }
    h = x.reshape(N_TOKENS, D_MODEL)
    for layer in range(DEPTH):
        j = layer // 2
        if layer % 2 == 0:
            x1, meta, cnt = _mixer_a_call(h, p, layer, j)
        else:
            x1, meta, cnt = _mixer_b_call(h, p, layer, j)
        off, pad_start, tile_blk, ea, eb, n_used = _tile_plan(cnt[:N_CLASSES, 0])
        xs = _dispatch_call(off, pad_start, n_used, meta, x1)
        ys = _moe_call(tile_blk, ea, eb, n_used, xs, p, layer)
        h = _gather_call(off, meta, ys)
    return h.reshape(BATCH, SEQ, D_MODEL)
```

```python
import types

import jax
import jax.numpy as jnp
from jax import lax
from jax.experimental import pallas as pl
from jax.experimental.pallas import tpu as pltpu

D_MODEL = 1024
BATCH = 16
SEQ = 4096
DEPTH = 2
CHUNK = 64
MIX_WIDTH = D_MODEL // 2
SGU_CHUNK = 128
SGU_HEADS = 4
SGU_HEAD_DIM = MIX_WIDTH // SGU_HEADS
CONV_WIDTH = 3
N_MEM = 256
XA_HEADS = 4
XA_HEAD_DIM = MIX_WIDTH // XA_HEADS
N_EXPERTS = 16
N_GROUPS = 4
GROUP_SIZE = N_EXPERTS // N_GROUPS
D_EXPERT = D_MODEL // 2
ALPHA = (2 * DEPTH) ** 0.25
LN_EPS = 1e-5

N_PAIRS = GROUP_SIZE * (GROUP_SIZE - 1) // 2
N_CLASSES = N_GROUPS * N_PAIRS
CLASS_ROWS = 32
PAIR_LO = (0, 0, 0, 1, 1, 2)
PAIR_HI = (1, 2, 3, 2, 3, 3)

LANES = 128
TM = 512
MOE_TILE = 256
ROW_W = D_MODEL + LANES
META_IDX, META_GA, META_GB = 0, 1, 2
HALO = 8
MIXER_ISSUE_CHUNKS = 8
MOE_ISSUE_CHUNKS = 4
VMEM_LIMIT = 56 * 1024 * 1024
KIND_FULL, KIND_PARTIAL, KIND_IDLE = 0, 1, 2

F32 = jnp.float32
BF16 = jnp.bfloat16
I32 = jnp.int32


def _cfg():
    n_tokens = BATCH * SEQ
    n_full_max = n_tokens // MOE_TILE
    n_pool_tiles = n_full_max + N_CLASSES
    assert n_pool_tiles <= TM and SEQ % TM == 0 and TM % SGU_CHUNK == 0
    return types.SimpleNamespace(
        n_tokens=n_tokens, n_mixer_steps=n_tokens // TM, tiles_per_seq=SEQ // TM,
        n_full_max=n_full_max, n_pool_tiles=n_pool_tiles, n_pool_rows=n_pool_tiles * MOE_TILE,
        n_moe_steps=n_full_max + 1 + N_CLASSES)


def _dot(a, b):
    return jnp.dot(a, b, preferred_element_type=F32)


def _dot_nt(a, b):
    return lax.dot_general(a, b, (((1,), (1,)), ((), ())), preferred_element_type=F32)


def _layer_norm(x, g, b):
    mu = jnp.mean(x, axis=-1, keepdims=True)
    xc = x - mu
    var = jnp.mean(xc * xc, axis=-1, keepdims=True)
    return xc * lax.rsqrt(var + LN_EPS) * g + b


def _sigmoid(x):
    return 1.0 / (1.0 + jnp.exp(-x))


def _row_copy(src_ref, src_row, dst_ref, dst_row, sem):
    return pltpu.make_async_copy(src_ref.at[pl.ds(src_row, 1)], dst_ref.at[pl.ds(dst_row, 1)], sem)


def _kv_kernel(mem_ref, wkv_ref, kt_ref, v_ref):
    kv = _dot(mem_ref[0].astype(BF16), wkv_ref[0])
    kt_ref[0, 0] = kv[:, :MIX_WIDTH].T.astype(BF16)
    v_ref[0, 0] = kv[:, MIX_WIDTH:].astype(BF16)


def _kv_call(mem, wkv_bf16):
    return pl.pallas_call(
        _kv_kernel,
        grid=(DEPTH, BATCH),
        in_specs=[
            pl.BlockSpec((1, N_MEM, D_MODEL), lambda l, b: (b, 0, 0)),
            pl.BlockSpec((1, D_MODEL, 2 * MIX_WIDTH), lambda l, b: (l, 0, 0)),
        ],
        out_specs=[
            pl.BlockSpec((1, 1, MIX_WIDTH, N_MEM), lambda l, b: (l, b, 0, 0)),
            pl.BlockSpec((1, 1, N_MEM, MIX_WIDTH), lambda l, b: (l, b, 0, 0)),
        ],
        out_shape=[
            jax.ShapeDtypeStruct((DEPTH, BATCH, MIX_WIDTH, N_MEM), BF16),
            jax.ShapeDtypeStruct((DEPTH, BATCH, N_MEM, MIX_WIDTH), BF16),
        ],
        name="kv_proj",
    )(mem, wkv_bf16)


def _route(y, step, rw_ref, rb_ref, tri_ref, low_ref, cnt_ref, cur_ref, nxt_ref, tbl_ref):
    y_hi = y.astype(BF16)
    y_lo = (y - y_hi.astype(F32)).astype(BF16)
    l_hi = _dot_nt(rw_ref[...], y_hi)
    l_lo = _dot_nt(rw_ref[0:32, :], y_lo)
    logits = l_hi[0:32] + l_hi[32:64] + l_lo
    scores = _sigmoid(logits)
    sel = scores + rb_ref[...]

    row = lax.broadcasted_iota(I32, (8, TM), 0).astype(F32)
    s = [sel[8 * m:8 * m + 8] for m in range(GROUP_SIZE)]
    sc = [scores[8 * m:8 * m + 8] for m in range(GROUP_SIZE)]
    neg = jnp.float32(-jnp.inf)
    m1 = jnp.maximum(jnp.maximum(s[0], s[1]), jnp.maximum(s[2], s[3]))
    i1 = jnp.where(s[0] == m1, 0.0, jnp.where(s[1] == m1, 1.0, jnp.where(s[2] == m1, 2.0, 3.0)))
    t = [jnp.where(i1 == float(m), neg, s[m]) for m in range(GROUP_SIZE)]
    m2 = jnp.maximum(jnp.maximum(t[0], t[1]), jnp.maximum(t[2], t[3]))
    i2 = jnp.where(t[0] == m2, 0.0, jnp.where(t[1] == m2, 1.0, jnp.where(t[2] == m2, 2.0, 3.0)))
    gscore = jnp.where(row < float(N_GROUPS), m1 + m2, neg)
    gmax = jnp.max(gscore, axis=0, keepdims=True)
    top_group = jnp.min(jnp.where(gscore == gmax, row, 8.0), axis=0, keepdims=True)
    lo = jnp.minimum(i1, i2)
    hi = jnp.maximum(i1, i2)
    pair = jnp.where(lo == 0.0, 0.0, jnp.where(lo == 1.0, 3.0, 5.0)) + hi - lo - 1.0
    s_lo = sum(jnp.where(lo == float(m), sc[m], 0.0) for m in range(GROUP_SIZE))
    s_hi = sum(jnp.where(hi == float(m), sc[m], 0.0) for m in range(GROUP_SIZE))
    inv = 1.0 / (s_lo + s_hi)
    chosen = row == top_group
    cls = jnp.sum(jnp.where(chosen, row * float(N_PAIRS) + pair, 0.0), axis=0, keepdims=True)
    ga = jnp.sum(jnp.where(chosen, s_lo * inv, 0.0), axis=0, keepdims=True)
    gb = jnp.sum(jnp.where(chosen, s_hi * inv, 0.0), axis=0, keepdims=True)

    crow = lax.broadcasted_iota(I32, (CLASS_ROWS, TM), 0).astype(F32)
    onehot = jnp.where(crow == cls, 1.0, 0.0)
    before = _dot(onehot.astype(BF16), tri_ref[...])
    cnt = cnt_ref[...]
    cur = cur_ref[...]
    nxt = nxt_ref[0:1, :]
    tile_f = float(MOE_TILE)
    inv_tile = 1.0 / MOE_TILE
    rank = cnt + before
    cnt_new = cnt + jnp.sum(onehot, axis=1, keepdims=True)
    tiles_old = jnp.floor((cnt + (tile_f - 1.0)) * inv_tile)
    tiles_new = jnp.floor((cnt_new + (tile_f - 1.0)) * inv_tile)
    fresh = tiles_new - tiles_old
    base = nxt + _dot(low_ref[...], fresh.astype(BF16))
    q = jnp.floor(rank * inv_tile)
    tile_of = jnp.where(q < tiles_old, cur, base + (q - tiles_old))
    pos_all = tile_of * tile_f + (rank - q * tile_f)
    pos = jnp.sum(onehot * pos_all, axis=0, keepdims=True)

    cnt_ref[...] = cnt_new
    cur_ref[...] = jnp.where(fresh > 0.0, base + fresh - 1.0, cur)
    nxt_ref[...] = jnp.broadcast_to(nxt + jnp.sum(fresh, axis=0, keepdims=True), nxt_ref.shape)
    ids = lax.broadcasted_iota(I32, (CLASS_ROWS, TM), 1).astype(F32)
    hit = jnp.where((ids >= base) & (ids < base + fresh), 1.0, 0.0)
    cls_hit = jnp.sum(hit * crow, axis=0, keepdims=True)
    any_hit = jnp.sum(hit, axis=0, keepdims=True)
    tbl_ref[...] = jnp.where(any_hit > 0.0, cls_hit, tbl_ref[...])

    idx = (lax.broadcasted_iota(I32, (1, TM), 1) + step * TM).astype(F32)
    return pos, idx, ga, gb


def _make_mixer_kernel(mixer, cfg):
    rows_per_chunk = TM // MIXER_ISSUE_CHUNKS

    def body(*refs):
        if mixer == "gmlp":
            (x_ref, win_ref, lng_ref, lnb_ref, sw_ref, sb_ref, *rest) = refs
        else:
            (x_ref, xh_ref, win_ref, cw_ref, *rest) = refs
        (kt_ref, v_ref, wout_ref, g1_ref, b1_ref, rw_ref, rb_ref,
         xs_ref, cnt_out, cur_out, nxt_out, tbl_out,
         tri_ref, low_ref, cnt_ref, cur_ref, nxt_ref, tbl_ref, stage_ref, posv_ref, poss_ref,
         stv_ref, sts_ref, scat_sem, pos_sem, st_sem) = rest

        step = pl.program_id(0)
        last = pl.num_programs(0) - 1
        slot = step % 2
        prev = 1 - slot

        @pl.when(step == 0)
        def _():
            r = lax.broadcasted_iota(I32, (TM, TM), 0)
            c = lax.broadcasted_iota(I32, (TM, TM), 1)
            tri_ref[...] = jnp.where(r < c, 1.0, 0.0).astype(BF16)
            r = lax.broadcasted_iota(I32, (CLASS_ROWS, CLASS_ROWS), 0)
            c = lax.broadcasted_iota(I32, (CLASS_ROWS, CLASS_ROWS), 1)
            low_ref[...] = jnp.where(c < r, 1.0, 0.0).astype(BF16)
            cnt_ref[...] = jnp.zeros_like(cnt_ref)
            cur_ref[...] = jnp.full_like(cur_ref, -1.0)
            nxt_ref[...] = jnp.zeros_like(nxt_ref)
            tbl_ref[...] = jnp.full_like(tbl_ref, -1.0)
            stage_ref[1] = jnp.zeros((TM, ROW_W), F32)

            def init_pos(r, carry):
                poss_ref[0, r] = r
                return carry

            lax.fori_loop(0, TM, init_pos, 0)

        @pl.when(step > 0)
        def _():
            pltpu.make_async_copy(posv_ref, poss_ref, pos_sem).wait()

        def issue(chunk):
            for r in range(chunk * rows_per_chunk, (chunk + 1) * rows_per_chunk):
                _row_copy(stage_ref.at[prev], r, xs_ref, poss_ref[0, r], scat_sem).start()

        x = x_ref[...]
        z = _dot(x.astype(BF16), win_ref[0])
        issue(0)
        if mixer == "gmlp":
            zuv = z[:, :2 * MIX_WIDTH]
            uv = zuv * (lax.erf(zuv * (2.0 ** -0.5)) + 1.0) * 0.5
            u = uv[:, :MIX_WIDTH]
            vn = _layer_norm(uv[:, MIX_WIDTH:], lng_ref[0], lnb_ref[0]).astype(BF16)
            issue(1)
            pos_r = lax.broadcasted_iota(I32, (SGU_CHUNK, SGU_CHUNK), 0) // CHUNK
            pos_c = lax.broadcasted_iota(I32, (SGU_CHUNK, SGU_CHUNK), 1) // CHUNK
            causal = pos_r >= pos_c
            rows = []
            for c in range(TM // SGU_CHUNK):
                blocks = []
                for h in range(SGU_HEADS):
                    w = jnp.where(causal, sw_ref[0, h], 0.0).astype(BF16)
                    blk = vn[c * SGU_CHUNK:(c + 1) * SGU_CHUNK, h * SGU_HEAD_DIM:(h + 1) * SGU_HEAD_DIM]
                    blocks.append(_dot(w, blk) + sb_ref[0, h])
                rows.append(jnp.concatenate(blocks, axis=1))
            tok = u * jnp.concatenate(rows, axis=0)
            q = z[:, 2 * MIX_WIDTH:]
        else:
            bg = z[:, :MIX_WIDTH]
            hcur = z[:, MIX_WIDTH:2 * MIX_WIDTH] * z[:, 2 * MIX_WIDTH:3 * MIX_WIDTH]
            zh = _dot(xh_ref[...].astype(BF16), win_ref[0, :, MIX_WIDTH:3 * MIX_WIDTH])
            first_of_seq = (step % cfg.tiles_per_seq) == 0
            hprev = jnp.where(first_of_seq, 0.0, zh[:, :MIX_WIDTH] * zh[:, MIX_WIDTH:])
            issue(1)
            rowid = lax.broadcasted_iota(I32, (TM, MIX_WIDTH), 0)
            h1 = jnp.where(rowid == 0, hprev[HALO - 1:HALO], pltpu.roll(hcur, 1, 0))
            h2 = jnp.where(rowid == 0, hprev[HALO - 2:HALO - 1],
                           jnp.where(rowid == 1, hprev[HALO - 1:HALO], pltpu.roll(hcur, 2, 0)))
            cw = cw_ref[0]
            tok = bg * (cw[0:1] * h2 + cw[1:2] * h1 + cw[2:3] * hcur)
            q = z[:, 3 * MIX_WIDTH:]
        issue(2)

        heads = []
        scale = XA_HEAD_DIM ** -0.5
        for h in range(XA_HEADS):
            lo, hi = h * XA_HEAD_DIM, (h + 1) * XA_HEAD_DIM
            sc = _dot(q[:, lo:hi].astype(BF16), kt_ref[0, 0, lo:hi, :]) * scale
            e = jnp.exp(sc - jnp.max(sc, axis=-1, keepdims=True))
            p = e * (1.0 / jnp.sum(e, axis=-1, keepdims=True))
            heads.append(_dot(p.astype(BF16), v_ref[0, 0, :, lo:hi]))
            issue(3 + h)
        cat = jnp.concatenate([tok] + heads, axis=1).astype(BF16)
        o = _dot(cat, wout_ref[0])
        issue(7)
        y = _layer_norm(ALPHA * x + o, g1_ref[0], b1_ref[0])

        pos, idx, ga, gb = _route(y, step, rw_ref, rb_ref, tri_ref, low_ref, cnt_ref, cur_ref, nxt_ref, tbl_ref)
        mrow = lax.broadcasted_iota(I32, (LANES, TM), 0)
        meta = jnp.where(mrow == META_IDX, idx,
                         jnp.where(mrow == META_GA, ga, jnp.where(mrow == META_GB, gb, 0.0)))
        stage_ref[slot, :, 0:D_MODEL] = y
        stage_ref[slot, :, D_MODEL:ROW_W] = jnp.transpose(meta)
        posv_ref[...] = pos.astype(I32)
        pltpu.make_async_copy(posv_ref, poss_ref, pos_sem).start()
        cnt_out[...] = cnt_ref[:, 0:LANES]
        cur_out[...] = cur_ref[:, 0:LANES]
        nxt_out[...] = nxt_ref[:, 0:LANES]
        tbl_out[...] = tbl_ref[...]
        pltpu.make_async_copy(stage_ref.at[prev], xs_ref.at[pl.ds(0, TM)], scat_sem).wait()

        @pl.when(step == last)
        def _():
            pltpu.make_async_copy(posv_ref, poss_ref, pos_sem).wait()

            def issue_own(r, carry):
                _row_copy(stage_ref.at[slot], r, xs_ref, poss_ref[0, r], scat_sem).start()
                return carry

            lax.fori_loop(0, TM, issue_own, 0, unroll=8)
            lane = lax.broadcasted_iota(I32, (CLASS_ROWS, LANES), 1).astype(F32)
            crow = lax.broadcasted_iota(I32, (CLASS_ROWS, LANES), 0).astype(F32)
            diag = crow == lane
            cnt_l = jnp.sum(jnp.where(diag, cnt_ref[:, 0:LANES], 0.0), axis=0, keepdims=True)
            cur_l = jnp.sum(jnp.where(diag, cur_ref[:, 0:LANES], 0.0), axis=0, keepdims=True)
            srow = lax.broadcasted_iota(I32, (8, LANES), 0)
            st = jnp.where(srow == 0, cnt_l, jnp.where(srow == 1, cur_l, nxt_ref[:, 0:LANES]))
            stv_ref[...] = st.astype(I32)
            cp = pltpu.make_async_copy(stv_ref, sts_ref, st_sem)
            cp.start()
            cp.wait()
            pltpu.make_async_copy(stage_ref.at[slot], xs_ref.at[pl.ds(0, TM)], scat_sem).wait()
            stage_ref[prev] = jnp.zeros((TM, ROW_W), F32)
            zrow = stage_ref.at[prev]

            def clear_class(c, carry):
                fill = sts_ref[0, c] % MOE_TILE
                start = sts_ref[1, c] * MOE_TILE + fill
                n = jnp.where(fill == 0, 0, MOE_TILE - fill)

                def one(j, carry2):
                    _row_copy(zrow, 0, xs_ref, start + j, scat_sem).start()
                    return carry2

                def one_wait(j, carry2):
                    _row_copy(zrow, 0, xs_ref, 0, scat_sem).wait()
                    return carry2

                lax.fori_loop(0, n, one, 0)
                lax.fori_loop(0, n, one_wait, 0)
                return carry

            lax.fori_loop(0, N_CLASSES, clear_class, 0)

            def clear_tile(t, carry):
                start = pl.multiple_of(t * MOE_TILE, MOE_TILE)
                cp2 = pltpu.make_async_copy(zrow.at[pl.ds(0, MOE_TILE)], xs_ref.at[pl.ds(start, MOE_TILE)],
                                            scat_sem)
                cp2.start()
                cp2.wait()
                return carry

            lax.fori_loop(sts_ref[2, 0], cfg.n_pool_tiles, clear_tile, 0)

    return body


def _mixer_call(mixer, x, p, layer, j):
    cfg = _cfg()

    def const_spec(shape):
        return pl.BlockSpec(shape, lambda i: (0,) * len(shape))

    def layer_spec(shape, l):
        return pl.BlockSpec((1,) + shape, lambda i: (l,) + (0,) * len(shape))

    x_spec = pl.BlockSpec((TM, D_MODEL), lambda i: (i, 0))
    if mixer == "gmlp":
        head_specs = [
            x_spec,
            layer_spec((D_MODEL, 3 * MIX_WIDTH), j),
            layer_spec((1, MIX_WIDTH), j),
            layer_spec((1, MIX_WIDTH), j),
            layer_spec((SGU_HEADS, SGU_CHUNK, SGU_CHUNK), j),
            layer_spec((SGU_HEADS, SGU_CHUNK, SGU_HEAD_DIM), j),
        ]
        head_args = (x, p["w_in_a"], p["sgu_ln_g"], p["sgu_ln_b"], p["sgu_w"], p["sgu_b_col"])
    else:
        head_specs = [
            x_spec,
            pl.BlockSpec((HALO, D_MODEL), lambda i: (jnp.maximum(i * (TM // HALO) - 1, 0), 0)),
            layer_spec((D_MODEL, 4 * MIX_WIDTH), j),
            layer_spec((CONV_WIDTH, MIX_WIDTH), j),
        ]
        head_args = (x, x, p["w_in_b"], p["conv_w"])
    tail_specs = [
        pl.BlockSpec((1, 1, MIX_WIDTH, N_MEM), lambda i: (layer, i // cfg.tiles_per_seq, 0, 0)),
        pl.BlockSpec((1, 1, N_MEM, MIX_WIDTH), lambda i: (layer, i // cfg.tiles_per_seq, 0, 0)),
        layer_spec((D_MODEL, D_MODEL), layer),
        layer_spec((1, D_MODEL), layer),
        layer_spec((1, D_MODEL), layer),
        const_spec((64, D_MODEL)),
        const_spec((32, 1)),
    ]
    tail_args = (p["kt"], p["v"], p["w_out"], p["ln1_g"], p["ln1_b"], p["rw"], p["rb"])
    return pl.pallas_call(
        _make_mixer_kernel(mixer, cfg),
        grid=(cfg.n_mixer_steps,),
        in_specs=head_specs + tail_specs,
        out_specs=[
            pl.BlockSpec(memory_space=pl.ANY),
            const_spec((CLASS_ROWS, LANES)),
            const_spec((CLASS_ROWS, LANES)),
            const_spec((8, LANES)),
            const_spec((8, TM)),
        ],
        out_shape=[
            jax.ShapeDtypeStruct((cfg.n_pool_rows, ROW_W), F32),
            jax.ShapeDtypeStruct((CLASS_ROWS, LANES), F32),
            jax.ShapeDtypeStruct((CLASS_ROWS, LANES), F32),
            jax.ShapeDtypeStruct((8, LANES), F32),
            jax.ShapeDtypeStruct((8, TM), F32),
        ],
        scratch_shapes=[
            pltpu.VMEM((TM, TM), BF16),
            pltpu.VMEM((CLASS_ROWS, CLASS_ROWS), BF16),
            pltpu.VMEM((CLASS_ROWS, TM), F32),
            pltpu.VMEM((CLASS_ROWS, TM), F32),
            pltpu.VMEM((8, TM), F32),
            pltpu.VMEM((8, TM), F32),
            pltpu.VMEM((2, TM, ROW_W), F32),
            pltpu.VMEM((1, TM), I32),
            pltpu.SMEM((1, TM), I32),
            pltpu.VMEM((8, LANES), I32),
            pltpu.SMEM((8, LANES), I32),
            pltpu.SemaphoreType.DMA(()),
            pltpu.SemaphoreType.DMA(()),
            pltpu.SemaphoreType.DMA(()),
        ],
        compiler_params=pltpu.CompilerParams(dimension_semantics=("arbitrary",), vmem_limit_bytes=VMEM_LIMIT),
        name="mixer_" + mixer,
    )(*head_args, *tail_args)


def _make_moe_kernel(cfg):
    rows_per_chunk = MOE_TILE // MOE_ISSUE_CHUNKS

    def body(blk_ref, ea_ref, eb_ref, kind_ref, nvalid_ref, nfull_ref,
             xs_ref, wgu_a_ref, wd_a_ref, wgu_b_ref, wd_b_ref, g2_ref, b2_ref, out_ref,
             stage_ref, idxv_ref, idxs_ref, scat_sem, idx_sem):
        step = pl.program_id(0)
        slot = step % 2
        prev = 1 - slot

        def experts(issue):
            x = xs_ref[:, 0:D_MODEL]
            xb = x.astype(BF16)
            meta = xs_ref[:, D_MODEL:ROW_W]
            outs = []
            for k, (wgu_ref, wd_ref) in enumerate(((wgu_a_ref, wd_a_ref), (wgu_b_ref, wd_b_ref))):
                gu = _dot(xb, wgu_ref[0, 0])
                issue(2 * k)
                g = gu[:, :D_EXPERT]
                h = g * _sigmoid(g) * gu[:, D_EXPERT:]
                outs.append(_dot(h.astype(BF16), wd_ref[0, 0]))
                issue(2 * k + 1)
            f = meta[:, META_GA:META_GA + 1] * outs[0] + meta[:, META_GB:META_GB + 1] * outs[1]
            y = _layer_norm(ALPHA * x + f, g2_ref[0], b2_ref[0])
            idx = jnp.transpose(meta)[META_IDX:META_IDX + 1, :]
            return y, idx

        def put_idx(idx):
            idxv_ref[...] = jnp.broadcast_to(idx, idxv_ref.shape).astype(I32)
            return pltpu.make_async_copy(idxv_ref, idxs_ref, idx_sem)

        @pl.when(step == 0)
        def _():
            stage_ref[1] = jnp.zeros((MOE_TILE, D_MODEL), F32)

            def init_idx(r, carry):
                idxs_ref[0, r] = r
                return carry

            lax.fori_loop(0, MOE_TILE, init_idx, 0)

        @pl.when(kind_ref[step] == KIND_FULL)
        def _():
            @pl.when(step > 0)
            def _():
                pltpu.make_async_copy(idxv_ref, idxs_ref, idx_sem).wait()

            def issue(chunk):
                for r in range(chunk * rows_per_chunk, (chunk + 1) * rows_per_chunk):
                    _row_copy(stage_ref.at[prev], r, out_ref, idxs_ref[0, r], scat_sem).start()

            y, idx = experts(issue)
            stage_ref[slot] = y

            @pl.when(step < nfull_ref[0])
            def _():
                put_idx(idx).start()

            pltpu.make_async_copy(stage_ref.at[prev], out_ref.at[pl.ds(0, MOE_TILE)], scat_sem).wait()

        @pl.when(kind_ref[step] == KIND_PARTIAL)
        def _():
            y, idx = experts(lambda chunk: None)
            stage_ref[0] = y
            cp = put_idx(idx)
            cp.start()
            cp.wait()
            n = nvalid_ref[step]

            def one(r, carry):
                _row_copy(stage_ref.at[0], r, out_ref, idxs_ref[0, r], scat_sem).start()
                return carry

            def one_wait(r, carry):
                _row_copy(stage_ref.at[0], 0, out_ref, 0, scat_sem).wait()
                return carry

            lax.fori_loop(0, n, one, 0)
            lax.fori_loop(0, n, one_wait, 0)

    return body


def _moe_call(plan, xs, p, layer):
    cfg = _cfg()

    def xmap(i, blk, *_):
        return (blk[i], 0)

    def wa(i, blk, ea, *_):
        return (layer, ea[i], 0, 0)

    def wb(i, blk, ea, eb, *_):
        return (layer, eb[i], 0, 0)

    def lmap(i, *_):
        return (layer, 0, 0)

    return pl.pallas_call(
        _make_moe_kernel(cfg),
        grid_spec=pltpu.PrefetchScalarGridSpec(
            num_scalar_prefetch=6,
            grid=(cfg.n_moe_steps,),
            in_specs=[
                pl.BlockSpec((MOE_TILE, ROW_W), xmap),
                pl.BlockSpec((1, 1, D_MODEL, 2 * D_EXPERT), wa),
                pl.BlockSpec((1, 1, D_EXPERT, D_MODEL), wa),
                pl.BlockSpec((1, 1, D_MODEL, 2 * D_EXPERT), wb),
                pl.BlockSpec((1, 1, D_EXPERT, D_MODEL), wb),
                pl.BlockSpec((1, 1, D_MODEL), lmap),
                pl.BlockSpec((1, 1, D_MODEL), lmap),
            ],
            out_specs=pl.BlockSpec(memory_space=pl.ANY),
            scratch_shapes=[
                pltpu.VMEM((2, MOE_TILE, D_MODEL), F32),
                pltpu.VMEM((8, MOE_TILE), I32),
                pltpu.SMEM((8, MOE_TILE), I32),
                pltpu.SemaphoreType.DMA(()),
                pltpu.SemaphoreType.DMA(()),
            ],
        ),
        out_shape=jax.ShapeDtypeStruct((cfg.n_tokens, D_MODEL), F32),
        compiler_params=pltpu.CompilerParams(dimension_semantics=("arbitrary",), vmem_limit_bytes=VMEM_LIMIT),
        name="moe_experts",
    )(*plan, xs, p["w_gu"], p["w_down"], p["w_gu"], p["w_down"], p["ln2_g"], p["ln2_b"])


def _moe_plan(cnt_out, cur_out, nxt_out, tbl_out):
    cfg = _cfg()
    cnt = cnt_out[:N_CLASSES, 0].astype(I32)
    cur = cur_out[:N_CLASSES, 0].astype(I32)
    n_used = nxt_out[0, 0].astype(I32)
    tile_cls = jnp.clip(tbl_out[0, :cfg.n_pool_tiles].astype(I32), 0, N_CLASSES - 1)
    tid = jnp.arange(cfg.n_pool_tiles, dtype=I32)
    fill = cnt % MOE_TILE
    ends_here = (cur[None, :] == tid[:, None]) & (fill[None, :] != 0)
    is_partial = jnp.any(ends_here, axis=1)
    n_valid = jnp.where(is_partial, jnp.sum(jnp.where(ends_here, fill[None, :], 0), axis=1), MOE_TILE)
    used = tid < n_used
    big = jnp.int32(1 << 30)
    key = tile_cls * (2 * cfg.n_pool_tiles) + tid
    full = used & ~is_partial
    part = used & is_partial
    def order_of(keys):
        rank = jnp.sum(keys[None, :] < keys[:, None], axis=1)
        return jnp.sum(jnp.where(rank[None, :] == tid[:, None], tid[None, :], 0), axis=1).astype(I32)

    order_full = order_of(jnp.where(full, key, big + tid))
    order_part = order_of(jnp.where(part, key, big + tid))
    n_full = jnp.sum(full).astype(I32)
    n_part = jnp.sum(part).astype(I32)
    last_full = order_full[n_full - 1]

    i = jnp.arange(cfg.n_full_max + 1, dtype=I32)
    blk_a = order_full[jnp.minimum(i, n_full - 1)]
    kind_a = jnp.where(i <= n_full, KIND_FULL, KIND_IDLE)
    k = jnp.arange(N_CLASSES, dtype=I32)
    blk_b = jnp.where(n_part > 0, order_part[jnp.clip(jnp.minimum(k, n_part - 1), 0, None)], last_full)
    kind_b = jnp.where(k < n_part, KIND_PARTIAL, KIND_IDLE)
    blk = jnp.concatenate([blk_a, blk_b]).astype(I32)
    kind = jnp.concatenate([kind_a, kind_b]).astype(I32)
    cls = tile_cls[blk]
    base = (cls // N_PAIRS) * GROUP_SIZE
    ea = base + jnp.asarray(PAIR_LO, I32)[cls % N_PAIRS]
    eb = base + jnp.asarray(PAIR_HI, I32)[cls % N_PAIRS]
    return blk, ea.astype(I32), eb.astype(I32), kind, n_valid[blk].astype(I32), n_full.reshape(1)


def _router_rows(router_w):
    wt = router_w.astype(F32).T.reshape(N_GROUPS, GROUP_SIZE, D_MODEL)
    wt = jnp.transpose(wt, (1, 0, 2))
    wt = jnp.pad(wt, ((0, 0), (0, 8 - N_GROUPS), (0, 0))).reshape(32, D_MODEL)
    hi = wt.astype(BF16)
    lo = (wt - hi.astype(F32)).astype(BF16)
    return jnp.concatenate([hi, lo], axis=0)


def _router_bias_rows(router_b):
    b = jnp.transpose(router_b.astype(F32).reshape(N_GROUPS, GROUP_SIZE), (1, 0))
    return jnp.pad(b, ((0, 0), (0, 8 - N_GROUPS))).reshape(32, 1)


def kernel(x, mem, w_in_a, sgu_ln_g, sgu_ln_b, sgu_w, sgu_b, w_in_b, conv_w, w_kv, w_out, ln1_g, ln1_b,
           router_w, router_b, w_gate, w_up, w_down, ln2_g, ln2_b):
    cfg = _cfg()
    kt, v = _kv_call(mem, w_kv.astype(BF16))
    p = {
        "w_in_a": w_in_a.astype(BF16),
        "sgu_ln_g": sgu_ln_g[:, None, :],
        "sgu_ln_b": sgu_ln_b[:, None, :],
        "sgu_w": sgu_w,
        "sgu_b_col": jnp.broadcast_to(sgu_b[..., None], sgu_b.shape + (SGU_HEAD_DIM,)),
        "w_in_b": w_in_b.astype(BF16),
        "conv_w": conv_w,
        "kt": kt,
        "v": v,
        "w_out": w_out.astype(BF16),
        "ln1_g": ln1_g[:, None, :],
        "ln1_b": ln1_b[:, None, :],
        "rw": _router_rows(router_w),
        "rb": _router_bias_rows(router_b),
        "w_gu": jnp.concatenate([w_gate, w_up], axis=-1).astype(BF16),
        "w_down": w_down.astype(BF16),
        "ln2_g": ln2_g[:, None, :],
        "ln2_b": ln2_b[:, None, :],
    }
    h = x.reshape(cfg.n_tokens, D_MODEL)
    for layer in range(DEPTH):
        mixer = "gmlp" if layer % 2 == 0 else "conv"
        xs, cnt, cur, nxt, tbl = _mixer_call(mixer, h, p, layer, layer // 2)
        h = _moe_call(_moe_plan(cnt, cur, nxt, tbl), xs, p, layer)
    return h.reshape(BATCH, SEQ, D_MODEL)
```

```python
import types

import jax
import jax.numpy as jnp
from jax import lax
from jax.experimental import pallas as pl
from jax.experimental.pallas import tpu as pltpu

D_MODEL = 1024
BATCH = 16
SEQ = 4096
DEPTH = 2
CHUNK = 64
MIX_WIDTH = D_MODEL // 2
SGU_CHUNK = 128
SGU_HEADS = 4
SGU_HEAD_DIM = MIX_WIDTH // SGU_HEADS
CONV_WIDTH = 3
N_MEM = 256
XA_HEADS = 4
XA_HEAD_DIM = MIX_WIDTH // XA_HEADS
N_EXPERTS = 16
N_GROUPS = 4
GROUP_SIZE = N_EXPERTS // N_GROUPS
D_EXPERT = D_MODEL // 2
ALPHA = (2 * DEPTH) ** 0.25
LN_EPS = 1e-5

N_PAIRS = GROUP_SIZE * (GROUP_SIZE - 1) // 2
N_CLASSES = N_GROUPS * N_PAIRS
CLASS_ROWS = 32
PAIR_LO = (0, 0, 0, 1, 1, 2)
PAIR_HI = (1, 2, 3, 2, 3, 3)

LANES = 128
TM = 512
MOE_TILE = 256
ROW_W = D_MODEL + LANES
META_IDX, META_GA, META_GB = 0, 1, 2
HALO = 8
MIXER_SLAB = 256
MIXER_ISSUE_ROWS = 32
MOE_SLAB = 256
VMEM_LIMIT = 56 * 1024 * 1024
KIND_FULL, KIND_PARTIAL, KIND_IDLE = 0, 1, 2

F32 = jnp.float32
BF16 = jnp.bfloat16
I32 = jnp.int32


def _cfg():
    n_tokens = BATCH * SEQ
    n_full_max = n_tokens // MOE_TILE
    n_pool_tiles = n_full_max + N_CLASSES
    assert n_pool_tiles <= TM and SEQ % TM == 0 and TM % SGU_CHUNK == 0
    return types.SimpleNamespace(
        n_tokens=n_tokens, n_mixer_steps=n_tokens // TM, tiles_per_seq=SEQ // TM,
        n_full_max=n_full_max, n_pool_tiles=n_pool_tiles, n_pool_rows=n_pool_tiles * MOE_TILE,
        n_moe_steps=n_full_max + 1 + N_CLASSES)


def _dot(a, b):
    return jnp.dot(a, b, preferred_element_type=F32)


def _dot_nt(a, b):
    return lax.dot_general(a, b, (((1,), (1,)), ((), ())), preferred_element_type=F32)


def _layer_norm(x, g, b):
    mu = jnp.mean(x, axis=-1, keepdims=True)
    xc = x - mu
    var = jnp.mean(xc * xc, axis=-1, keepdims=True)
    return xc * lax.rsqrt(var + LN_EPS) * g + b


def _sigmoid(x):
    return 1.0 / (1.0 + jnp.exp(-x))


def _row_copy(src_ref, src_row, dst_ref, dst_row, sem):
    return pltpu.make_async_copy(src_ref.at[pl.ds(src_row, 1)], dst_ref.at[pl.ds(dst_row, 1)], sem)


def _kv_kernel(mem_ref, wkv_ref, kt_ref, v_ref):
    kv = _dot(mem_ref[0].astype(BF16), wkv_ref[0])
    kt_ref[0, 0] = kv[:, :MIX_WIDTH].T.astype(BF16)
    v_ref[0, 0] = kv[:, MIX_WIDTH:].astype(BF16)


def _kv_call(mem, wkv_bf16):
    return pl.pallas_call(
        _kv_kernel,
        grid=(DEPTH, BATCH),
        in_specs=[
            pl.BlockSpec((1, N_MEM, D_MODEL), lambda l, b: (b, 0, 0)),
            pl.BlockSpec((1, D_MODEL, 2 * MIX_WIDTH), lambda l, b: (l, 0, 0)),
        ],
        out_specs=[
            pl.BlockSpec((1, 1, MIX_WIDTH, N_MEM), lambda l, b: (l, b, 0, 0)),
            pl.BlockSpec((1, 1, N_MEM, MIX_WIDTH), lambda l, b: (l, b, 0, 0)),
        ],
        out_shape=[
            jax.ShapeDtypeStruct((DEPTH, BATCH, MIX_WIDTH, N_MEM), BF16),
            jax.ShapeDtypeStruct((DEPTH, BATCH, N_MEM, MIX_WIDTH), BF16),
        ],
        name="kv_proj",
    )(mem, wkv_bf16)


def _route(y, step, rw_ref, rb_ref, tri_ref, low_ref, cnt_ref, cur_ref, nxt_ref, tbl_ref):
    y_hi = y.astype(BF16)
    y_lo = (y - y_hi.astype(F32)).astype(BF16)
    l_hi = _dot_nt(rw_ref[...], y_hi)
    l_lo = _dot_nt(rw_ref[0:32, :], y_lo)
    logits = l_hi[0:32] + l_hi[32:64] + l_lo
    scores = _sigmoid(logits)
    sel = scores + rb_ref[...]

    row = lax.broadcasted_iota(I32, (8, TM), 0).astype(F32)
    s = [sel[8 * m:8 * m + 8] for m in range(GROUP_SIZE)]
    sc = [scores[8 * m:8 * m + 8] for m in range(GROUP_SIZE)]
    neg = jnp.float32(-jnp.inf)
    m1 = jnp.maximum(jnp.maximum(s[0], s[1]), jnp.maximum(s[2], s[3]))
    i1 = jnp.where(s[0] == m1, 0.0, jnp.where(s[1] == m1, 1.0, jnp.where(s[2] == m1, 2.0, 3.0)))
    t = [jnp.where(i1 == float(m), neg, s[m]) for m in range(GROUP_SIZE)]
    m2 = jnp.maximum(jnp.maximum(t[0], t[1]), jnp.maximum(t[2], t[3]))
    i2 = jnp.where(t[0] == m2, 0.0, jnp.where(t[1] == m2, 1.0, jnp.where(t[2] == m2, 2.0, 3.0)))
    gscore = jnp.where(row < float(N_GROUPS), m1 + m2, neg)
    gmax = jnp.max(gscore, axis=0, keepdims=True)
    top_group = jnp.min(jnp.where(gscore == gmax, row, 8.0), axis=0, keepdims=True)
    lo = jnp.minimum(i1, i2)
    hi = jnp.maximum(i1, i2)
    pair = jnp.where(lo == 0.0, 0.0, jnp.where(lo == 1.0, 3.0, 5.0)) + hi - lo - 1.0
    s_lo = sum(jnp.where(lo == float(m), sc[m], 0.0) for m in range(GROUP_SIZE))
    s_hi = sum(jnp.where(hi == float(m), sc[m], 0.0) for m in range(GROUP_SIZE))
    inv = 1.0 / (s_lo + s_hi)
    chosen = row == top_group
    cls = jnp.sum(jnp.where(chosen, row * float(N_PAIRS) + pair, 0.0), axis=0, keepdims=True)
    ga = jnp.sum(jnp.where(chosen, s_lo * inv, 0.0), axis=0, keepdims=True)
    gb = jnp.sum(jnp.where(chosen, s_hi * inv, 0.0), axis=0, keepdims=True)

    crow = lax.broadcasted_iota(I32, (CLASS_ROWS, TM), 0).astype(F32)
    onehot = jnp.where(crow == cls, 1.0, 0.0)
    before = _dot(onehot.astype(BF16), tri_ref[...])
    cnt = cnt_ref[...]
    cur = cur_ref[...]
    nxt = nxt_ref[0:1, :]
    tile_f = float(MOE_TILE)
    inv_tile = 1.0 / MOE_TILE
    rank = cnt + before
    cnt_new = cnt + jnp.sum(onehot, axis=1, keepdims=True)
    tiles_old = jnp.floor((cnt + (tile_f - 1.0)) * inv_tile)
    tiles_new = jnp.floor((cnt_new + (tile_f - 1.0)) * inv_tile)
    fresh = tiles_new - tiles_old
    base = nxt + _dot(low_ref[...], fresh.astype(BF16))
    q = jnp.floor(rank * inv_tile)
    tile_of = jnp.where(q < tiles_old, cur, base + (q - tiles_old))
    pos_all = tile_of * tile_f + (rank - q * tile_f)
    pos = jnp.sum(onehot * pos_all, axis=0, keepdims=True)

    cnt_ref[...] = cnt_new
    cur_ref[...] = jnp.where(fresh > 0.0, base + fresh - 1.0, cur)
    nxt_ref[...] = jnp.broadcast_to(nxt + jnp.sum(fresh, axis=0, keepdims=True), nxt_ref.shape)
    ids = lax.broadcasted_iota(I32, (CLASS_ROWS, TM), 1).astype(F32)
    hit = jnp.where((ids >= base) & (ids < base + fresh), 1.0, 0.0)
    cls_hit = jnp.sum(hit * crow, axis=0, keepdims=True)
    any_hit = jnp.sum(hit, axis=0, keepdims=True)
    tbl_ref[...] = jnp.where(any_hit > 0.0, cls_hit, tbl_ref[...])

    idx = (lax.broadcasted_iota(I32, (1, TM), 1) + step * TM).astype(F32)
    return pos, idx, ga, gb


def _make_mixer_kernel(mixer, cfg):
    def body(*refs):
        if mixer == "gmlp":
            (x_ref, win_ref, lng_ref, lnb_ref, sw_ref, sb_ref, *rest) = refs
        else:
            (x_ref, xh_ref, win_ref, cw_ref, *rest) = refs
        (kt_ref, v_ref, wout_ref, g1_ref, b1_ref, rw_ref, rb_ref,
         xs_ref, cnt_out, cur_out, nxt_out, tbl_out,
         tri_ref, low_ref, cnt_ref, cur_ref, nxt_ref, tbl_ref, buf_ref, posv_ref, poss_ref,
         stv_ref, sts_ref, scat_sem, pos_sem, st_sem) = rest

        step = pl.program_id(0)
        last = pl.num_programs(0) - 1
        slot = step % 2
        prev = 1 - slot
        zbuf = buf_ref.at[pl.ds(2 * TM, TM)]
        catbuf = buf_ref.at[pl.ds(3 * TM, TM), pl.ds(0, D_MODEL)]
        obuf = buf_ref.at[pl.ds(3 * TM, TM), pl.ds(D_MODEL, D_MODEL)]

        def stage_copy(stage, r):
            src = buf_ref.at[pl.ds(stage * TM + r, 1), pl.ds(0, ROW_W)]
            return pltpu.make_async_copy(src, xs_ref.at[pl.ds(poss_ref[0, r], 1)], scat_sem)

        def stage_wait():
            pltpu.make_async_copy(buf_ref.at[pl.ds(0, TM), pl.ds(0, ROW_W)], xs_ref.at[pl.ds(0, TM)],
                                  scat_sem).wait()

        @pl.when(step == 0)
        def _():
            r = lax.broadcasted_iota(I32, (TM, TM), 0)
            c = lax.broadcasted_iota(I32, (TM, TM), 1)
            tri_ref[...] = jnp.where(r < c, 1.0, 0.0).astype(BF16)
            r = lax.broadcasted_iota(I32, (CLASS_ROWS, CLASS_ROWS), 0)
            c = lax.broadcasted_iota(I32, (CLASS_ROWS, CLASS_ROWS), 1)
            low_ref[...] = jnp.where(c < r, 1.0, 0.0).astype(BF16)
            cnt_ref[...] = jnp.zeros_like(cnt_ref)
            cur_ref[...] = jnp.full_like(cur_ref, -1.0)
            nxt_ref[...] = jnp.zeros_like(nxt_ref)
            tbl_ref[...] = jnp.full_like(tbl_ref, -1.0)
            buf_ref[pl.ds(TM, TM), 0:ROW_W] = jnp.zeros((TM, ROW_W), F32)

            def init_pos(r, carry):
                poss_ref[0, r] = r
                return carry

            lax.fori_loop(0, TM, init_pos, 0)

        @pl.when(step > 0)
        def _():
            pltpu.make_async_copy(posv_ref, poss_ref, pos_sem).wait()

        issued = [0]

        def issue_next():
            lo = issued[0]
            hi = min(lo + MIXER_ISSUE_ROWS, TM)
            for r in range(lo, hi):
                stage_copy(prev, r).start()
            issued[0] = hi

        x = x_ref[...]
        xb = x.astype(BF16)
        for n in range(win_ref.shape[-1] // MIXER_SLAB):
            cols = slice(n * MIXER_SLAB, (n + 1) * MIXER_SLAB)
            issue_next()
            zbuf[:, cols] = _dot(xb, win_ref[0, :, cols])
        if mixer == "gmlp":
            zuv = zbuf[:, 0:2 * MIX_WIDTH]
            uv = zuv * (lax.erf(zuv * (2.0 ** -0.5)) + 1.0) * 0.5
            u = uv[:, :MIX_WIDTH]
            vn = _layer_norm(uv[:, MIX_WIDTH:], lng_ref[0], lnb_ref[0]).astype(BF16)
            pos_r = lax.broadcasted_iota(I32, (SGU_CHUNK, SGU_CHUNK), 0) // CHUNK
            pos_c = lax.broadcasted_iota(I32, (SGU_CHUNK, SGU_CHUNK), 1) // CHUNK
            causal = pos_r >= pos_c
            for c in range(TM // SGU_CHUNK):
                rows = slice(c * SGU_CHUNK, (c + 1) * SGU_CHUNK)
                blocks = []
                for h in range(SGU_HEADS):
                    w = jnp.where(causal, sw_ref[0, h], 0.0).astype(BF16)
                    blk = vn[rows, h * SGU_HEAD_DIM:(h + 1) * SGU_HEAD_DIM]
                    blocks.append(_dot(w, blk) + sb_ref[0, h])
                issue_next()
                catbuf[rows, 0:MIX_WIDTH] = u[rows] * jnp.concatenate(blocks, axis=1)
            q = zbuf[:, 2 * MIX_WIDTH:3 * MIX_WIDTH]
        else:
            bg = zbuf[:, 0:MIX_WIDTH]
            hcur = zbuf[:, MIX_WIDTH:2 * MIX_WIDTH] * zbuf[:, 2 * MIX_WIDTH:3 * MIX_WIDTH]
            zh = _dot(xh_ref[...].astype(BF16), win_ref[0, :, MIX_WIDTH:3 * MIX_WIDTH])
            first_of_seq = (step % cfg.tiles_per_seq) == 0
            hprev = jnp.where(first_of_seq, 0.0, zh[:, :MIX_WIDTH] * zh[:, MIX_WIDTH:])
            rowid = lax.broadcasted_iota(I32, (TM, MIX_WIDTH), 0)
            h1 = jnp.where(rowid == 0, hprev[HALO - 1:HALO], pltpu.roll(hcur, 1, 0))
            h2 = jnp.where(rowid == 0, hprev[HALO - 2:HALO - 1],
                           jnp.where(rowid == 1, hprev[HALO - 1:HALO], pltpu.roll(hcur, 2, 0)))
            cw = cw_ref[0]
            issue_next()
            catbuf[:, 0:MIX_WIDTH] = bg * (cw[0:1] * h2 + cw[1:2] * h1 + cw[2:3] * hcur)
            q = zbuf[:, 3 * MIX_WIDTH:4 * MIX_WIDTH]

        scale = XA_HEAD_DIM ** -0.5
        for h in range(XA_HEADS):
            lo, hi = h * XA_HEAD_DIM, (h + 1) * XA_HEAD_DIM
            sc = _dot(q[:, lo:hi].astype(BF16), kt_ref[0, 0, lo:hi, :]) * scale
            e = jnp.exp(sc - jnp.max(sc, axis=-1, keepdims=True))
            p = e * (1.0 / jnp.sum(e, axis=-1, keepdims=True))
            issue_next()
            catbuf[:, MIX_WIDTH + lo:MIX_WIDTH + hi] = _dot(p.astype(BF16), v_ref[0, 0, :, lo:hi])
        cat = catbuf[:, 0:D_MODEL].astype(BF16)
        for n in range(D_MODEL // MIXER_SLAB):
            cols = slice(n * MIXER_SLAB, (n + 1) * MIXER_SLAB)
            issue_next()
            obuf[:, cols] = _dot(cat, wout_ref[0, :, cols])
        while issued[0] < TM:
            issue_next()
        y = _layer_norm(ALPHA * x + obuf[...], g1_ref[0], b1_ref[0])

        pos, idx, ga, gb = _route(y, step, rw_ref, rb_ref, tri_ref, low_ref, cnt_ref, cur_ref, nxt_ref, tbl_ref)
        mrow = lax.broadcasted_iota(I32, (LANES, TM), 0)
        meta = jnp.where(mrow == META_IDX, idx,
                         jnp.where(mrow == META_GA, ga, jnp.where(mrow == META_GB, gb, 0.0)))
        own = pl.ds(pl.multiple_of(slot * TM, TM), TM)
        buf_ref[own, 0:D_MODEL] = y
        buf_ref[own, D_MODEL:ROW_W] = jnp.transpose(meta)
        posv_ref[...] = pos.astype(I32)
        pltpu.make_async_copy(posv_ref, poss_ref, pos_sem).start()
        cnt_out[...] = cnt_ref[:, 0:LANES]
        cur_out[...] = cur_ref[:, 0:LANES]
        nxt_out[...] = nxt_ref[:, 0:LANES]
        tbl_out[...] = tbl_ref[...]
        stage_wait()

        @pl.when(step == last)
        def _():
            pltpu.make_async_copy(posv_ref, poss_ref, pos_sem).wait()

            def issue_own(r, carry):
                stage_copy(slot, r).start()
                return carry

            lax.fori_loop(0, TM, issue_own, 0, unroll=8)
            lane = lax.broadcasted_iota(I32, (CLASS_ROWS, LANES), 1).astype(F32)
            crow = lax.broadcasted_iota(I32, (CLASS_ROWS, LANES), 0).astype(F32)
            diag = crow == lane
            cnt_l = jnp.sum(jnp.where(diag, cnt_ref[:, 0:LANES], 0.0), axis=0, keepdims=True)
            cur_l = jnp.sum(jnp.where(diag, cur_ref[:, 0:LANES], 0.0), axis=0, keepdims=True)
            srow = lax.broadcasted_iota(I32, (8, LANES), 0)
            st = jnp.where(srow == 0, cnt_l, jnp.where(srow == 1, cur_l, nxt_ref[:, 0:LANES]))
            stv_ref[...] = st.astype(I32)
            cp = pltpu.make_async_copy(stv_ref, sts_ref, st_sem)
            cp.start()
            cp.wait()
            stage_wait()
            zero_row = pl.multiple_of(prev * TM, TM)
            buf_ref[pl.ds(zero_row, TM), 0:ROW_W] = jnp.zeros((TM, ROW_W), F32)
            zrow = buf_ref.at[pl.ds(zero_row, MOE_TILE), pl.ds(0, ROW_W)]

            def clear_class(c, carry):
                fill = sts_ref[0, c] % MOE_TILE
                start = sts_ref[1, c] * MOE_TILE + fill
                n = jnp.where(fill == 0, 0, MOE_TILE - fill)

                def one(j, carry2):
                    _row_copy(zrow, 0, xs_ref, start + j, scat_sem).start()
                    return carry2

                def one_wait(j, carry2):
                    _row_copy(zrow, 0, xs_ref, 0, scat_sem).wait()
                    return carry2

                lax.fori_loop(0, n, one, 0)
                lax.fori_loop(0, n, one_wait, 0)
                return carry

            lax.fori_loop(0, N_CLASSES, clear_class, 0)

            def clear_tile(t, carry):
                start = pl.multiple_of(t * MOE_TILE, MOE_TILE)
                cp2 = pltpu.make_async_copy(zrow, xs_ref.at[pl.ds(start, MOE_TILE)], scat_sem)
                cp2.start()
                cp2.wait()
                return carry

            lax.fori_loop(sts_ref[2, 0], cfg.n_pool_tiles, clear_tile, 0)

    return body


def _mixer_call(mixer, x, p, layer, j):
    cfg = _cfg()

    def const_spec(shape):
        return pl.BlockSpec(shape, lambda i: (0,) * len(shape))

    def layer_spec(shape, l):
        return pl.BlockSpec((1,) + shape, lambda i: (l,) + (0,) * len(shape))

    x_spec = pl.BlockSpec((TM, D_MODEL), lambda i: (i, 0))
    if mixer == "gmlp":
        head_specs = [
            x_spec,
            layer_spec((D_MODEL, 3 * MIX_WIDTH), j),
            layer_spec((1, MIX_WIDTH), j),
            layer_spec((1, MIX_WIDTH), j),
            layer_spec((SGU_HEADS, SGU_CHUNK, SGU_CHUNK), j),
            layer_spec((SGU_HEADS, SGU_CHUNK, SGU_HEAD_DIM), j),
        ]
        head_args = (x, p["w_in_a"], p["sgu_ln_g"], p["sgu_ln_b"], p["sgu_w"], p["sgu_b_col"])
    else:
        head_specs = [
            x_spec,
            pl.BlockSpec((HALO, D_MODEL), lambda i: (jnp.maximum(i * (TM // HALO) - 1, 0), 0)),
            layer_spec((D_MODEL, 4 * MIX_WIDTH), j),
            layer_spec((CONV_WIDTH, MIX_WIDTH), j),
        ]
        head_args = (x, x, p["w_in_b"], p["conv_w"])
    tail_specs = [
        pl.BlockSpec((1, 1, MIX_WIDTH, N_MEM), lambda i: (layer, i // cfg.tiles_per_seq, 0, 0)),
        pl.BlockSpec((1, 1, N_MEM, MIX_WIDTH), lambda i: (layer, i // cfg.tiles_per_seq, 0, 0)),
        layer_spec((D_MODEL, D_MODEL), layer),
        layer_spec((1, D_MODEL), layer),
        layer_spec((1, D_MODEL), layer),
        const_spec((64, D_MODEL)),
        const_spec((32, 1)),
    ]
    tail_args = (p["kt"], p["v"], p["w_out"], p["ln1_g"], p["ln1_b"], p["rw"], p["rb"])
    return pl.pallas_call(
        _make_mixer_kernel(mixer, cfg),
        grid=(cfg.n_mixer_steps,),
        in_specs=head_specs + tail_specs,
        out_specs=[
            pl.BlockSpec(memory_space=pl.ANY),
            const_spec((CLASS_ROWS, LANES)),
            const_spec((CLASS_ROWS, LANES)),
            const_spec((8, LANES)),
            const_spec((8, TM)),
        ],
        out_shape=[
            jax.ShapeDtypeStruct((cfg.n_pool_rows, ROW_W), F32),
            jax.ShapeDtypeStruct((CLASS_ROWS, LANES), F32),
            jax.ShapeDtypeStruct((CLASS_ROWS, LANES), F32),
            jax.ShapeDtypeStruct((8, LANES), F32),
            jax.ShapeDtypeStruct((8, TM), F32),
        ],
        scratch_shapes=[
            pltpu.VMEM((TM, TM), BF16),
            pltpu.VMEM((CLASS_ROWS, CLASS_ROWS), BF16),
            pltpu.VMEM((CLASS_ROWS, TM), F32),
            pltpu.VMEM((CLASS_ROWS, TM), F32),
            pltpu.VMEM((8, TM), F32),
            pltpu.VMEM((8, TM), F32),
            pltpu.VMEM((4 * TM, 2 * D_MODEL), F32),
            pltpu.VMEM((1, TM), I32),
            pltpu.SMEM((1, TM), I32),
            pltpu.VMEM((8, LANES), I32),
            pltpu.SMEM((8, LANES), I32),
            pltpu.SemaphoreType.DMA(()),
            pltpu.SemaphoreType.DMA(()),
            pltpu.SemaphoreType.DMA(()),
        ],
        compiler_params=pltpu.CompilerParams(dimension_semantics=("arbitrary",), vmem_limit_bytes=VMEM_LIMIT),
        name="mixer_" + mixer,
    )(*head_args, *tail_args)


def _make_moe_kernel(cfg):
    n_slabs = D_MODEL // MOE_SLAB
    rows_per_chunk = MOE_TILE // (4 * n_slabs)
    stage_row = (0, MOE_TILE)
    work_row, facc_row = 2 * MOE_TILE, 3 * MOE_TILE

    def body(blk_ref, ea_ref, eb_ref, kind_ref, nvalid_ref, nfull_ref,
             xs_ref, wgu_a_ref, wd_a_ref, wgu_b_ref, wd_b_ref, g2_ref, b2_ref, out_ref,
             buf_ref, idxv_ref, idxs_ref, scat_sem, idx_sem):
        step = pl.program_id(0)
        slot = step % 2
        prev = 1 - slot
        work = buf_ref.at[pl.ds(work_row, MOE_TILE)]
        facc = buf_ref.at[pl.ds(facc_row, MOE_TILE)]

        def stage_copy(stage, r, sem):
            return _row_copy(buf_ref, stage * MOE_TILE + r, out_ref, idxs_ref[0, r], sem)

        def experts(issue):
            x = xs_ref[:, 0:D_MODEL]
            xb = x.astype(BF16)
            meta = xs_ref[:, D_MODEL:ROW_W]
            gates = (meta[:, META_GA:META_GA + 1], meta[:, META_GB:META_GB + 1])
            chunk = 0
            for k, (wgu_ref, wd_ref) in enumerate(((wgu_a_ref, wd_a_ref), (wgu_b_ref, wd_b_ref))):
                for n in range(n_slabs):
                    cols = slice(n * MOE_SLAB, (n + 1) * MOE_SLAB)
                    issue(chunk)
                    chunk += 1
                    work[:, cols] = _dot(xb, wgu_ref[0, 0, :, cols])
                g = work[:, :D_EXPERT]
                hb = (g * _sigmoid(g) * work[:, D_EXPERT:]).astype(BF16)
                for n in range(n_slabs):
                    cols = slice(n * MOE_SLAB, (n + 1) * MOE_SLAB)
                    issue(chunk)
                    chunk += 1
                    part = gates[k] * _dot(hb, wd_ref[0, 0, :, cols])
                    facc[:, cols] = part if k == 0 else facc[:, cols] + part
            y = _layer_norm(ALPHA * x + facc[...], g2_ref[0], b2_ref[0])
            idx = jnp.transpose(meta)[META_IDX:META_IDX + 1, :]
            return y, idx

        def put_idx(idx):
            idxv_ref[...] = jnp.broadcast_to(idx, idxv_ref.shape).astype(I32)
            return pltpu.make_async_copy(idxv_ref, idxs_ref, idx_sem)

        @pl.when(step == 0)
        def _():
            buf_ref[pl.ds(stage_row[1], MOE_TILE), :] = jnp.zeros((MOE_TILE, D_MODEL), F32)

            def init_idx(r, carry):
                idxs_ref[0, r] = r
                return carry

            lax.fori_loop(0, MOE_TILE, init_idx, 0)

        @pl.when(kind_ref[step] == KIND_FULL)
        def _():
            @pl.when(step > 0)
            def _():
                pltpu.make_async_copy(idxv_ref, idxs_ref, idx_sem).wait()

            def issue(chunk):
                for r in range(chunk * rows_per_chunk, (chunk + 1) * rows_per_chunk):
                    stage_copy(prev, r, scat_sem).start()

            y, idx = experts(issue)
            buf_ref[pl.ds(pl.multiple_of(slot * MOE_TILE, MOE_TILE), MOE_TILE), :] = y

            @pl.when(step < nfull_ref[0])
            def _():
                put_idx(idx).start()

            pltpu.make_async_copy(buf_ref.at[pl.ds(0, MOE_TILE)], out_ref.at[pl.ds(0, MOE_TILE)], scat_sem).wait()

        @pl.when(kind_ref[step] == KIND_PARTIAL)
        def _():
            y, idx = experts(lambda chunk: None)
            buf_ref[pl.ds(stage_row[0], MOE_TILE), :] = y
            cp = put_idx(idx)
            cp.start()
            cp.wait()
            n = nvalid_ref[step]

            def one(r, carry):
                stage_copy(0, r, scat_sem).start()
                return carry

            def one_wait(r, carry):
                _row_copy(buf_ref, 0, out_ref, 0, scat_sem).wait()
                return carry

            lax.fori_loop(0, n, one, 0)
            lax.fori_loop(0, n, one_wait, 0)

    return body


def _moe_call(plan, xs, p, layer):
    cfg = _cfg()

    def xmap(i, blk, *_):
        return (blk[i], 0)

    def wa(i, blk, ea, *_):
        return (layer, ea[i], 0, 0)

    def wb(i, blk, ea, eb, *_):
        return (layer, eb[i], 0, 0)

    def lmap(i, *_):
        return (layer, 0, 0)

    return pl.pallas_call(
        _make_moe_kernel(cfg),
        grid_spec=pltpu.PrefetchScalarGridSpec(
            num_scalar_prefetch=6,
            grid=(cfg.n_moe_steps,),
            in_specs=[
                pl.BlockSpec((MOE_TILE, ROW_W), xmap),
                pl.BlockSpec((1, 1, D_MODEL, 2 * D_EXPERT), wa),
                pl.BlockSpec((1, 1, D_EXPERT, D_MODEL), wa),
                pl.BlockSpec((1, 1, D_MODEL, 2 * D_EXPERT), wb),
                pl.BlockSpec((1, 1, D_EXPERT, D_MODEL), wb),
                pl.BlockSpec((1, 1, D_MODEL), lmap),
                pl.BlockSpec((1, 1, D_MODEL), lmap),
            ],
            out_specs=pl.BlockSpec(memory_space=pl.ANY),
            scratch_shapes=[
                pltpu.VMEM((4 * MOE_TILE, D_MODEL), F32),
                pltpu.VMEM((8, MOE_TILE), I32),
                pltpu.SMEM((8, MOE_TILE), I32),
                pltpu.SemaphoreType.DMA(()),
                pltpu.SemaphoreType.DMA(()),
            ],
        ),
        out_shape=jax.ShapeDtypeStruct((cfg.n_tokens, D_MODEL), F32),
        compiler_params=pltpu.CompilerParams(dimension_semantics=("arbitrary",), vmem_limit_bytes=VMEM_LIMIT),
        name="moe_experts",
    )(*plan, xs, p["w_gu"], p["w_down"], p["w_gu"], p["w_down"], p["ln2_g"], p["ln2_b"])


def _moe_plan(cnt_out, cur_out, nxt_out, tbl_out):
    cfg = _cfg()
    cnt = cnt_out[:N_CLASSES, 0].astype(I32)
    cur = cur_out[:N_CLASSES, 0].astype(I32)
    n_used = nxt_out[0, 0].astype(I32)
    tile_cls = jnp.clip(tbl_out[0, :cfg.n_pool_tiles].astype(I32), 0, N_CLASSES - 1)
    tid = jnp.arange(cfg.n_pool_tiles, dtype=I32)
    fill = cnt % MOE_TILE
    ends_here = (cur[None, :] == tid[:, None]) & (fill[None, :] != 0)
    is_partial = jnp.any(ends_here, axis=1)
    n_valid = jnp.where(is_partial, jnp.sum(jnp.where(ends_here, fill[None, :], 0), axis=1), MOE_TILE)
    used = tid < n_used
    big = jnp.int32(1 << 30)
    key = tile_cls * (2 * cfg.n_pool_tiles) + tid
    full = used & ~is_partial
    part = used & is_partial
    def order_of(keys):
        rank = jnp.sum(keys[None, :] < keys[:, None], axis=1)
        return jnp.sum(jnp.where(rank[None, :] == tid[:, None], tid[None, :], 0), axis=1).astype(I32)

    order_full = order_of(jnp.where(full, key, big + tid))
    order_part = order_of(jnp.where(part, key, big + tid))
    n_full = jnp.sum(full).astype(I32)
    n_part = jnp.sum(part).astype(I32)
    last_full = order_full[n_full - 1]

    i = jnp.arange(cfg.n_full_max + 1, dtype=I32)
    blk_a = order_full[jnp.minimum(i, n_full - 1)]
    kind_a = jnp.where(i <= n_full, KIND_FULL, KIND_IDLE)
    k = jnp.arange(N_CLASSES, dtype=I32)
    blk_b = jnp.where(n_part > 0, order_part[jnp.clip(jnp.minimum(k, n_part - 1), 0, None)], last_full)
    kind_b = jnp.where(k < n_part, KIND_PARTIAL, KIND_IDLE)
    blk = jnp.concatenate([blk_a, blk_b]).astype(I32)
    kind = jnp.concatenate([kind_a, kind_b]).astype(I32)
    cls = tile_cls[blk]
    base = (cls // N_PAIRS) * GROUP_SIZE
    ea = base + jnp.asarray(PAIR_LO, I32)[cls % N_PAIRS]
    eb = base + jnp.asarray(PAIR_HI, I32)[cls % N_PAIRS]
    return blk, ea.astype(I32), eb.astype(I32), kind, n_valid[blk].astype(I32), n_full.reshape(1)


def _router_rows(router_w):
    wt = router_w.astype(F32).T.reshape(N_GROUPS, GROUP_SIZE, D_MODEL)
    wt = jnp.transpose(wt, (1, 0, 2))
    wt = jnp.pad(wt, ((0, 0), (0, 8 - N_GROUPS), (0, 0))).reshape(32, D_MODEL)
    hi = wt.astype(BF16)
    lo = (wt - hi.astype(F32)).astype(BF16)
    return jnp.concatenate([hi, lo], axis=0)


def _router_bias_rows(router_b):
    b = jnp.transpose(router_b.astype(F32).reshape(N_GROUPS, GROUP_SIZE), (1, 0))
    return jnp.pad(b, ((0, 0), (0, 8 - N_GROUPS))).reshape(32, 1)


def kernel(x, mem, w_in_a, sgu_ln_g, sgu_ln_b, sgu_w, sgu_b, w_in_b, conv_w, w_kv, w_out, ln1_g, ln1_b,
           router_w, router_b, w_gate, w_up, w_down, ln2_g, ln2_b):
    cfg = _cfg()
    kt, v = _kv_call(mem, w_kv.astype(BF16))
    p = {
        "w_in_a": w_in_a.astype(BF16),
        "sgu_ln_g": sgu_ln_g[:, None, :],
        "sgu_ln_b": sgu_ln_b[:, None, :],
        "sgu_w": sgu_w,
        "sgu_b_col": jnp.broadcast_to(sgu_b[..., None], sgu_b.shape + (SGU_HEAD_DIM,)),
        "w_in_b": w_in_b.astype(BF16),
        "conv_w": conv_w,
        "kt": kt,
        "v": v,
        "w_out": w_out.astype(BF16),
        "ln1_g": ln1_g[:, None, :],
        "ln1_b": ln1_b[:, None, :],
        "rw": _router_rows(router_w),
        "rb": _router_bias_rows(router_b),
        "w_gu": jnp.concatenate([w_gate, w_up], axis=-1).astype(BF16),
        "w_down": w_down.astype(BF16),
        "ln2_g": ln2_g[:, None, :],
        "ln2_b": ln2_b[:, None, :],
    }
    h = x.reshape(cfg.n_tokens, D_MODEL)
    for layer in range(DEPTH):
        mixer = "gmlp" if layer % 2 == 0 else "conv"
        xs, cnt, cur, nxt, tbl = _mixer_call(mixer, h, p, layer, layer // 2)
        h = _moe_call(_moe_plan(cnt, cur, nxt, tbl), xs, p, layer)
    return h.reshape(BATCH, SEQ, D_MODEL)
```

```python
import types

import jax
import jax.numpy as jnp
from jax import lax
from jax.experimental import pallas as pl
from jax.experimental.pallas import tpu as pltpu

D_MODEL = 1024
BATCH = 16
SEQ = 4096
DEPTH = 2
CHUNK = 64
MIX_WIDTH = D_MODEL // 2
SGU_CHUNK = 128
SGU_HEADS = 4
SGU_HEAD_DIM = MIX_WIDTH // SGU_HEADS
CONV_WIDTH = 3
N_MEM = 256
XA_HEADS = 4
XA_HEAD_DIM = MIX_WIDTH // XA_HEADS
N_EXPERTS = 16
N_GROUPS = 4
GROUP_SIZE = N_EXPERTS // N_GROUPS
D_EXPERT = D_MODEL // 2
ALPHA = (2 * DEPTH) ** 0.25
LN_EPS = 1e-5

N_PAIRS = GROUP_SIZE * (GROUP_SIZE - 1) // 2
N_CLASSES = N_GROUPS * N_PAIRS
CLASS_ROWS = 32
PAIR_LO = (0, 0, 0, 1, 1, 2)
PAIR_HI = (1, 2, 3, 2, 3, 3)

LANES = 128
TM = 512
MOE_TILE = 256
ROW_W = D_MODEL + LANES
META_IDX, META_GA, META_GB = 0, 1, 2
HALO = 8
N_STAGES = 3
MIXER_SLAB = 256
MIXER_ISSUE_ROWS = 32
MOE_SLAB = 256
VMEM_LIMIT = 56 * 1024 * 1024
KIND_FULL, KIND_PARTIAL, KIND_IDLE = 0, 1, 2

F32 = jnp.float32
BF16 = jnp.bfloat16
I32 = jnp.int32


def _cfg():
    n_tokens = BATCH * SEQ
    n_full_max = n_tokens // MOE_TILE
    n_pool_tiles = n_full_max + N_CLASSES
    assert n_pool_tiles <= TM and SEQ % TM == 0 and TM % SGU_CHUNK == 0
    assert n_tokens // TM >= N_STAGES and n_tokens - N_CLASSES * (MOE_TILE - 1) >= N_STAGES * MOE_TILE
    return types.SimpleNamespace(
        n_tokens=n_tokens, n_mixer_steps=n_tokens // TM, tiles_per_seq=SEQ // TM,
        n_full_max=n_full_max, n_pool_tiles=n_pool_tiles, n_pool_rows=n_pool_tiles * MOE_TILE,
        n_moe_steps=n_full_max + 1 + N_CLASSES)


def _dot(a, b):
    return jnp.dot(a, b, preferred_element_type=F32)


def _dot_nt(a, b):
    return lax.dot_general(a, b, (((1,), (1,)), ((), ())), preferred_element_type=F32)


def _layer_norm(x, g, b):
    mu = jnp.mean(x, axis=-1, keepdims=True)
    xc = x - mu
    var = jnp.mean(xc * xc, axis=-1, keepdims=True)
    return xc * lax.rsqrt(var + LN_EPS) * g + b


def _sigmoid(x):
    return 1.0 / (1.0 + jnp.exp(-x))


def _row_copy(src_ref, src_row, dst_ref, dst_row, sem):
    return pltpu.make_async_copy(src_ref.at[pl.ds(src_row, 1)], dst_ref.at[pl.ds(dst_row, 1)], sem)


def _kv_kernel(mem_ref, wkv_ref, kt_ref, v_ref):
    kv = _dot(mem_ref[0].astype(BF16), wkv_ref[0])
    kt_ref[0, 0] = kv[:, :MIX_WIDTH].T.astype(BF16)
    v_ref[0, 0] = kv[:, MIX_WIDTH:].astype(BF16)


def _kv_call(mem, wkv_bf16):
    return pl.pallas_call(
        _kv_kernel,
        grid=(DEPTH, BATCH),
        in_specs=[
            pl.BlockSpec((1, N_MEM, D_MODEL), lambda l, b: (b, 0, 0)),
            pl.BlockSpec((1, D_MODEL, 2 * MIX_WIDTH), lambda l, b: (l, 0, 0)),
        ],
        out_specs=[
            pl.BlockSpec((1, 1, MIX_WIDTH, N_MEM), lambda l, b: (l, b, 0, 0)),
            pl.BlockSpec((1, 1, N_MEM, MIX_WIDTH), lambda l, b: (l, b, 0, 0)),
        ],
        out_shape=[
            jax.ShapeDtypeStruct((DEPTH, BATCH, MIX_WIDTH, N_MEM), BF16),
            jax.ShapeDtypeStruct((DEPTH, BATCH, N_MEM, MIX_WIDTH), BF16),
        ],
        name="kv_proj",
    )(mem, wkv_bf16)


def _route(y, step, rw_ref, rb_ref, tri_ref, low_ref, cnt_ref, cur_ref, nxt_ref, tbl_ref):
    y_hi = y.astype(BF16)
    y_lo = (y - y_hi.astype(F32)).astype(BF16)
    l_hi = _dot_nt(rw_ref[...], y_hi)
    l_lo = _dot_nt(rw_ref[0:32, :], y_lo)
    logits = l_hi[0:32] + l_hi[32:64] + l_lo
    scores = _sigmoid(logits)
    sel = scores + rb_ref[...]

    row = lax.broadcasted_iota(I32, (8, TM), 0).astype(F32)
    s = [sel[8 * m:8 * m + 8] for m in range(GROUP_SIZE)]
    sc = [scores[8 * m:8 * m + 8] for m in range(GROUP_SIZE)]
    neg = jnp.float32(-jnp.inf)
    m1 = jnp.maximum(jnp.maximum(s[0], s[1]), jnp.maximum(s[2], s[3]))
    i1 = jnp.where(s[0] == m1, 0.0, jnp.where(s[1] == m1, 1.0, jnp.where(s[2] == m1, 2.0, 3.0)))
    t = [jnp.where(i1 == float(m), neg, s[m]) for m in range(GROUP_SIZE)]
    m2 = jnp.maximum(jnp.maximum(t[0], t[1]), jnp.maximum(t[2], t[3]))
    i2 = jnp.where(t[0] == m2, 0.0, jnp.where(t[1] == m2, 1.0, jnp.where(t[2] == m2, 2.0, 3.0)))
    gscore = jnp.where(row < float(N_GROUPS), m1 + m2, neg)
    gmax = jnp.max(gscore, axis=0, keepdims=True)
    top_group = jnp.min(jnp.where(gscore == gmax, row, 8.0), axis=0, keepdims=True)
    lo = jnp.minimum(i1, i2)
    hi = jnp.maximum(i1, i2)
    pair = jnp.where(lo == 0.0, 0.0, jnp.where(lo == 1.0, 3.0, 5.0)) + hi - lo - 1.0
    s_lo = sum(jnp.where(lo == float(m), sc[m], 0.0) for m in range(GROUP_SIZE))
    s_hi = sum(jnp.where(hi == float(m), sc[m], 0.0) for m in range(GROUP_SIZE))
    inv = 1.0 / (s_lo + s_hi)
    chosen = row == top_group
    cls = jnp.sum(jnp.where(chosen, row * float(N_PAIRS) + pair, 0.0), axis=0, keepdims=True)
    ga = jnp.sum(jnp.where(chosen, s_lo * inv, 0.0), axis=0, keepdims=True)
    gb = jnp.sum(jnp.where(chosen, s_hi * inv, 0.0), axis=0, keepdims=True)

    crow = lax.broadcasted_iota(I32, (CLASS_ROWS, TM), 0).astype(F32)
    onehot = jnp.where(crow == cls, 1.0, 0.0)
    before = _dot(onehot.astype(BF16), tri_ref[...])
    cnt = cnt_ref[...]
    cur = cur_ref[...]
    nxt = nxt_ref[0:1, :]
    tile_f = float(MOE_TILE)
    inv_tile = 1.0 / MOE_TILE
    rank = cnt + before
    cnt_new = cnt + jnp.sum(onehot, axis=1, keepdims=True)
    tiles_old = jnp.floor((cnt + (tile_f - 1.0)) * inv_tile)
    tiles_new = jnp.floor((cnt_new + (tile_f - 1.0)) * inv_tile)
    fresh = tiles_new - tiles_old
    base = nxt + _dot(low_ref[...], fresh.astype(BF16))
    q = jnp.floor(rank * inv_tile)
    tile_of = jnp.where(q < tiles_old, cur, base + (q - tiles_old))
    pos_all = tile_of * tile_f + (rank - q * tile_f)
    pos = jnp.sum(onehot * pos_all, axis=0, keepdims=True)

    cnt_ref[...] = cnt_new
    cur_ref[...] = jnp.where(fresh > 0.0, base + fresh - 1.0, cur)
    nxt_ref[...] = jnp.broadcast_to(nxt + jnp.sum(fresh, axis=0, keepdims=True), nxt_ref.shape)
    ids = lax.broadcasted_iota(I32, (CLASS_ROWS, TM), 1).astype(F32)
    hit = jnp.where((ids >= base) & (ids < base + fresh), 1.0, 0.0)
    cls_hit = jnp.sum(hit * crow, axis=0, keepdims=True)
    any_hit = jnp.sum(hit, axis=0, keepdims=True)
    tbl_ref[...] = jnp.where(any_hit > 0.0, cls_hit, tbl_ref[...])

    idx = (lax.broadcasted_iota(I32, (1, TM), 1) + step * TM).astype(F32)
    return pos, idx, ga, gb


def _make_mixer_kernel(mixer, cfg):
    def body(*refs):
        if mixer == "gmlp":
            (x_ref, win_ref, lng_ref, lnb_ref, sw_ref, sb_ref, *rest) = refs
        else:
            (x_ref, xh_ref, win_ref, cw_ref, *rest) = refs
        (kt_ref, v_ref, wout_ref, g1_ref, b1_ref, rw_ref, rb_ref,
         xs_ref, cnt_out, cur_out, nxt_out, tbl_out,
         tri_ref, low_ref, cnt_ref, cur_ref, nxt_ref, tbl_ref, buf_ref, posv_ref, poss_ref,
         stv_ref, sts_ref, scat_sem, pos_sem, st_sem) = rest

        step = pl.program_id(0)
        last = pl.num_programs(0) - 1
        slot = step % N_STAGES
        prev = (step + N_STAGES - 1) % N_STAGES
        zbuf = buf_ref.at[pl.ds(N_STAGES * TM, TM)]
        catbuf = buf_ref.at[pl.ds((N_STAGES + 1) * TM, TM), pl.ds(0, D_MODEL)]
        obuf = buf_ref.at[pl.ds((N_STAGES + 1) * TM, TM), pl.ds(D_MODEL, D_MODEL)]

        def stage_copy(stage, r):
            src = buf_ref.at[pl.ds(stage * TM + r, 1), pl.ds(0, ROW_W)]
            return pltpu.make_async_copy(src, xs_ref.at[pl.ds(poss_ref[0, r], 1)], scat_sem.at[stage])

        def stage_wait(stage):
            pltpu.make_async_copy(buf_ref.at[pl.ds(0, TM), pl.ds(0, ROW_W)], xs_ref.at[pl.ds(0, TM)],
                                  scat_sem.at[stage]).wait()

        @pl.when(step == 0)
        def _():
            r = lax.broadcasted_iota(I32, (TM, TM), 0)
            c = lax.broadcasted_iota(I32, (TM, TM), 1)
            tri_ref[...] = jnp.where(r < c, 1.0, 0.0).astype(BF16)
            r = lax.broadcasted_iota(I32, (CLASS_ROWS, CLASS_ROWS), 0)
            c = lax.broadcasted_iota(I32, (CLASS_ROWS, CLASS_ROWS), 1)
            low_ref[...] = jnp.where(c < r, 1.0, 0.0).astype(BF16)
            cnt_ref[...] = jnp.zeros_like(cnt_ref)
            cur_ref[...] = jnp.full_like(cur_ref, -1.0)
            nxt_ref[...] = jnp.zeros_like(nxt_ref)
            tbl_ref[...] = jnp.full_like(tbl_ref, -1.0)
            buf_ref[pl.ds((N_STAGES - 1) * TM, TM), 0:ROW_W] = jnp.zeros((TM, ROW_W), F32)

            def init_pos(r, carry):
                poss_ref[0, r] = r
                return carry

            lax.fori_loop(0, TM, init_pos, 0)

        @pl.when(step > 0)
        def _():
            pltpu.make_async_copy(posv_ref, poss_ref, pos_sem).wait()

        issued = [0]

        def issue_next():
            lo = issued[0]
            hi = min(lo + MIXER_ISSUE_ROWS, TM)
            for r in range(lo, hi):
                stage_copy(prev, r).start()
            issued[0] = hi

        x = x_ref[...]
        xb = x.astype(BF16)
        for n in range(win_ref.shape[-1] // MIXER_SLAB):
            cols = slice(n * MIXER_SLAB, (n + 1) * MIXER_SLAB)
            issue_next()
            zbuf[:, cols] = _dot(xb, win_ref[0, :, cols])
        if mixer == "gmlp":
            zuv = zbuf[:, 0:2 * MIX_WIDTH]
            uv = zuv * (lax.erf(zuv * (2.0 ** -0.5)) + 1.0) * 0.5
            u = uv[:, :MIX_WIDTH]
            vn = _layer_norm(uv[:, MIX_WIDTH:], lng_ref[0], lnb_ref[0]).astype(BF16)
            pos_r = lax.broadcasted_iota(I32, (SGU_CHUNK, SGU_CHUNK), 0) // CHUNK
            pos_c = lax.broadcasted_iota(I32, (SGU_CHUNK, SGU_CHUNK), 1) // CHUNK
            causal = pos_r >= pos_c
            for c in range(TM // SGU_CHUNK):
                rows = slice(c * SGU_CHUNK, (c + 1) * SGU_CHUNK)
                blocks = []
                for h in range(SGU_HEADS):
                    w = jnp.where(causal, sw_ref[0, h], 0.0).astype(BF16)
                    blk = vn[rows, h * SGU_HEAD_DIM:(h + 1) * SGU_HEAD_DIM]
                    blocks.append(_dot(w, blk) + sb_ref[0, h])
                issue_next()
                catbuf[rows, 0:MIX_WIDTH] = u[rows] * jnp.concatenate(blocks, axis=1)
            q = zbuf[:, 2 * MIX_WIDTH:3 * MIX_WIDTH]
        else:
            bg = zbuf[:, 0:MIX_WIDTH]
            hcur = zbuf[:, MIX_WIDTH:2 * MIX_WIDTH] * zbuf[:, 2 * MIX_WIDTH:3 * MIX_WIDTH]
            zh = _dot(xh_ref[...].astype(BF16), win_ref[0, :, MIX_WIDTH:3 * MIX_WIDTH])
            first_of_seq = (step % cfg.tiles_per_seq) == 0
            hprev = jnp.where(first_of_seq, 0.0, zh[:, :MIX_WIDTH] * zh[:, MIX_WIDTH:])
            rowid = lax.broadcasted_iota(I32, (TM, MIX_WIDTH), 0)
            h1 = jnp.where(rowid == 0, hprev[HALO - 1:HALO], pltpu.roll(hcur, 1, 0))
            h2 = jnp.where(rowid == 0, hprev[HALO - 2:HALO - 1],
                           jnp.where(rowid == 1, hprev[HALO - 1:HALO], pltpu.roll(hcur, 2, 0)))
            cw = cw_ref[0]
            issue_next()
            catbuf[:, 0:MIX_WIDTH] = bg * (cw[0:1] * h2 + cw[1:2] * h1 + cw[2:3] * hcur)
            q = zbuf[:, 3 * MIX_WIDTH:4 * MIX_WIDTH]

        scale = XA_HEAD_DIM ** -0.5
        for h in range(XA_HEADS):
            lo, hi = h * XA_HEAD_DIM, (h + 1) * XA_HEAD_DIM
            sc = _dot(q[:, lo:hi].astype(BF16), kt_ref[0, 0, lo:hi, :]) * scale
            e = jnp.exp(sc - jnp.max(sc, axis=-1, keepdims=True))
            p = e * (1.0 / jnp.sum(e, axis=-1, keepdims=True))
            issue_next()
            catbuf[:, MIX_WIDTH + lo:MIX_WIDTH + hi] = _dot(p.astype(BF16), v_ref[0, 0, :, lo:hi])
        cat = catbuf[:, 0:D_MODEL].astype(BF16)
        for n in range(D_MODEL // MIXER_SLAB):
            cols = slice(n * MIXER_SLAB, (n + 1) * MIXER_SLAB)
            issue_next()
            obuf[:, cols] = _dot(cat, wout_ref[0, :, cols])
        while issued[0] < TM:
            issue_next()
        y = _layer_norm(ALPHA * x + obuf[...], g1_ref[0], b1_ref[0])

        pos, idx, ga, gb = _route(y, step, rw_ref, rb_ref, tri_ref, low_ref, cnt_ref, cur_ref, nxt_ref, tbl_ref)
        mrow = lax.broadcasted_iota(I32, (LANES, TM), 0)
        meta = jnp.where(mrow == META_IDX, idx,
                         jnp.where(mrow == META_GA, ga, jnp.where(mrow == META_GB, gb, 0.0)))
        meta_t = jnp.transpose(meta)
        posv_ref[...] = pos.astype(I32)
        cnt_out[...] = cnt_ref[:, 0:LANES]
        cur_out[...] = cur_ref[:, 0:LANES]
        nxt_out[...] = nxt_ref[:, 0:LANES]
        tbl_out[...] = tbl_ref[...]

        @pl.when(step == 0)
        def _():
            stage_wait(prev)

        @pl.when(step >= N_STAGES)
        def _():
            stage_wait(slot)

        own = pl.ds(pl.multiple_of(slot * TM, TM), TM)
        buf_ref[own, 0:D_MODEL] = y
        buf_ref[own, D_MODEL:ROW_W] = meta_t
        pltpu.make_async_copy(posv_ref, poss_ref, pos_sem).start()

        @pl.when(step == last)
        def _():
            pltpu.make_async_copy(posv_ref, poss_ref, pos_sem).wait()

            def issue_own(r, carry):
                stage_copy(slot, r).start()
                return carry

            lax.fori_loop(0, TM, issue_own, 0, unroll=8)
            lane = lax.broadcasted_iota(I32, (CLASS_ROWS, LANES), 1).astype(F32)
            crow = lax.broadcasted_iota(I32, (CLASS_ROWS, LANES), 0).astype(F32)
            diag = crow == lane
            cnt_l = jnp.sum(jnp.where(diag, cnt_ref[:, 0:LANES], 0.0), axis=0, keepdims=True)
            cur_l = jnp.sum(jnp.where(diag, cur_ref[:, 0:LANES], 0.0), axis=0, keepdims=True)
            srow = lax.broadcasted_iota(I32, (8, LANES), 0)
            st = jnp.where(srow == 0, cnt_l, jnp.where(srow == 1, cur_l, nxt_ref[:, 0:LANES]))
            stv_ref[...] = st.astype(I32)
            cp = pltpu.make_async_copy(stv_ref, sts_ref, st_sem)
            cp.start()
            cp.wait()
            for stage in range(N_STAGES):
                stage_wait((step + stage) % N_STAGES)
            zero_row = pl.multiple_of(prev * TM, TM)
            buf_ref[pl.ds(zero_row, TM), 0:ROW_W] = jnp.zeros((TM, ROW_W), F32)
            zrow = buf_ref.at[pl.ds(zero_row, MOE_TILE), pl.ds(0, ROW_W)]
            zsem = scat_sem.at[0]

            def clear_class(c, carry):
                fill = sts_ref[0, c] % MOE_TILE
                start = sts_ref[1, c] * MOE_TILE + fill
                n = jnp.where(fill == 0, 0, MOE_TILE - fill)

                def one(j, carry2):
                    _row_copy(zrow, 0, xs_ref, start + j, zsem).start()
                    return carry2

                def one_wait(j, carry2):
                    _row_copy(zrow, 0, xs_ref, 0, zsem).wait()
                    return carry2

                lax.fori_loop(0, n, one, 0)
                lax.fori_loop(0, n, one_wait, 0)
                return carry

            lax.fori_loop(0, N_CLASSES, clear_class, 0)

            def clear_tile(t, carry):
                start = pl.multiple_of(t * MOE_TILE, MOE_TILE)
                cp2 = pltpu.make_async_copy(zrow, xs_ref.at[pl.ds(start, MOE_TILE)], zsem)
                cp2.start()
                cp2.wait()
                return carry

            lax.fori_loop(sts_ref[2, 0], cfg.n_pool_tiles, clear_tile, 0)

    return body


def _mixer_call(mixer, x, p, layer, j):
    cfg = _cfg()

    def const_spec(shape):
        return pl.BlockSpec(shape, lambda i: (0,) * len(shape))

    def layer_spec(shape, l):
        return pl.BlockSpec((1,) + shape, lambda i: (l,) + (0,) * len(shape))

    x_spec = pl.BlockSpec((TM, D_MODEL), lambda i: (i, 0))
    if mixer == "gmlp":
        head_specs = [
            x_spec,
            layer_spec((D_MODEL, 3 * MIX_WIDTH), j),
            layer_spec((1, MIX_WIDTH), j),
            layer_spec((1, MIX_WIDTH), j),
            layer_spec((SGU_HEADS, SGU_CHUNK, SGU_CHUNK), j),
            layer_spec((SGU_HEADS, SGU_CHUNK, SGU_HEAD_DIM), j),
        ]
        head_args = (x, p["w_in_a"], p["sgu_ln_g"], p["sgu_ln_b"], p["sgu_w"], p["sgu_b_col"])
    else:
        head_specs = [
            x_spec,
            pl.BlockSpec((HALO, D_MODEL), lambda i: (jnp.maximum(i * (TM // HALO) - 1, 0), 0)),
            layer_spec((D_MODEL, 4 * MIX_WIDTH), j),
            layer_spec((CONV_WIDTH, MIX_WIDTH), j),
        ]
        head_args = (x, x, p["w_in_b"], p["conv_w"])
    tail_specs = [
        pl.BlockSpec((1, 1, MIX_WIDTH, N_MEM), lambda i: (layer, i // cfg.tiles_per_seq, 0, 0)),
        pl.BlockSpec((1, 1, N_MEM, MIX_WIDTH), lambda i: (layer, i // cfg.tiles_per_seq, 0, 0)),
        layer_spec((D_MODEL, D_MODEL), layer),
        layer_spec((1, D_MODEL), layer),
        layer_spec((1, D_MODEL), layer),
        const_spec((64, D_MODEL)),
        const_spec((32, 1)),
    ]
    tail_args = (p["kt"], p["v"], p["w_out"], p["ln1_g"], p["ln1_b"], p["rw"], p["rb"])
    return pl.pallas_call(
        _make_mixer_kernel(mixer, cfg),
        grid=(cfg.n_mixer_steps,),
        in_specs=head_specs + tail_specs,
        out_specs=[
            pl.BlockSpec(memory_space=pl.ANY),
            const_spec((CLASS_ROWS, LANES)),
            const_spec((CLASS_ROWS, LANES)),
            const_spec((8, LANES)),
            const_spec((8, TM)),
        ],
        out_shape=[
            jax.ShapeDtypeStruct((cfg.n_pool_rows, ROW_W), F32),
            jax.ShapeDtypeStruct((CLASS_ROWS, LANES), F32),
            jax.ShapeDtypeStruct((CLASS_ROWS, LANES), F32),
            jax.ShapeDtypeStruct((8, LANES), F32),
            jax.ShapeDtypeStruct((8, TM), F32),
        ],
        scratch_shapes=[
            pltpu.VMEM((TM, TM), BF16),
            pltpu.VMEM((CLASS_ROWS, CLASS_ROWS), BF16),
            pltpu.VMEM((CLASS_ROWS, TM), F32),
            pltpu.VMEM((CLASS_ROWS, TM), F32),
            pltpu.VMEM((8, TM), F32),
            pltpu.VMEM((8, TM), F32),
            pltpu.VMEM(((N_STAGES + 2) * TM, 2 * D_MODEL), F32),
            pltpu.VMEM((1, TM), I32),
            pltpu.SMEM((1, TM), I32),
            pltpu.VMEM((8, LANES), I32),
            pltpu.SMEM((8, LANES), I32),
            pltpu.SemaphoreType.DMA((N_STAGES,)),
            pltpu.SemaphoreType.DMA(()),
            pltpu.SemaphoreType.DMA(()),
        ],
        compiler_params=pltpu.CompilerParams(dimension_semantics=("arbitrary",), vmem_limit_bytes=VMEM_LIMIT),
        name="mixer_" + mixer,
    )(*head_args, *tail_args)


def _make_moe_kernel(cfg):
    n_slabs = D_MODEL // MOE_SLAB
    rows_per_chunk = MOE_TILE // (4 * n_slabs)
    work_row, facc_row = N_STAGES * MOE_TILE, (N_STAGES + 1) * MOE_TILE

    def body(blk_ref, ea_ref, eb_ref, kind_ref, nvalid_ref, nfull_ref,
             xs_ref, wgu_a_ref, wd_a_ref, wgu_b_ref, wd_b_ref, g2_ref, b2_ref, out_ref,
             buf_ref, idxv_ref, idxs_ref, scat_sem, idx_sem):
        step = pl.program_id(0)
        slot = step % N_STAGES
        prev = (step + N_STAGES - 1) % N_STAGES
        work = buf_ref.at[pl.ds(work_row, MOE_TILE)]
        facc = buf_ref.at[pl.ds(facc_row, MOE_TILE)]

        def stage_copy(stage, r):
            return _row_copy(buf_ref, stage * MOE_TILE + r, out_ref, idxs_ref[0, r], scat_sem.at[stage])

        def stage_wait(stage):
            pltpu.make_async_copy(buf_ref.at[pl.ds(0, MOE_TILE)], out_ref.at[pl.ds(0, MOE_TILE)],
                                  scat_sem.at[stage]).wait()

        def experts(issue):
            x = xs_ref[:, 0:D_MODEL]
            xb = x.astype(BF16)
            meta = xs_ref[:, D_MODEL:ROW_W]
            gates = (meta[:, META_GA:META_GA + 1], meta[:, META_GB:META_GB + 1])
            chunk = 0
            for k, (wgu_ref, wd_ref) in enumerate(((wgu_a_ref, wd_a_ref), (wgu_b_ref, wd_b_ref))):
                for n in range(n_slabs):
                    cols = slice(n * MOE_SLAB, (n + 1) * MOE_SLAB)
                    issue(chunk)
                    chunk += 1
                    work[:, cols] = _dot(xb, wgu_ref[0, 0, :, cols])
                g = work[:, :D_EXPERT]
                hb = (g * _sigmoid(g) * work[:, D_EXPERT:]).astype(BF16)
                for n in range(n_slabs):
                    cols = slice(n * MOE_SLAB, (n + 1) * MOE_SLAB)
                    issue(chunk)
                    chunk += 1
                    part = gates[k] * _dot(hb, wd_ref[0, 0, :, cols])
                    facc[:, cols] = part if k == 0 else facc[:, cols] + part
            y = _layer_norm(ALPHA * x + facc[...], g2_ref[0], b2_ref[0])
            idx = jnp.transpose(meta)[META_IDX:META_IDX + 1, :]
            return y, idx

        def put_idx(idx):
            idxv_ref[...] = jnp.broadcast_to(idx, idxv_ref.shape).astype(I32)
            return pltpu.make_async_copy(idxv_ref, idxs_ref, idx_sem)

        @pl.when(step == 0)
        def _():
            buf_ref[pl.ds((N_STAGES - 1) * MOE_TILE, MOE_TILE), :] = jnp.zeros((MOE_TILE, D_MODEL), F32)

            def init_idx(r, carry):
                idxs_ref[0, r] = r
                return carry

            lax.fori_loop(0, MOE_TILE, init_idx, 0)

        @pl.when(kind_ref[step] == KIND_FULL)
        def _():
            @pl.when(step > 0)
            def _():
                pltpu.make_async_copy(idxv_ref, idxs_ref, idx_sem).wait()

            def issue(chunk):
                for r in range(chunk * rows_per_chunk, (chunk + 1) * rows_per_chunk):
                    stage_copy(prev, r).start()

            y, idx = experts(issue)

            @pl.when(step == 0)
            def _():
                stage_wait(prev)

            @pl.when(step >= N_STAGES)
            def _():
                stage_wait(slot)

            buf_ref[pl.ds(pl.multiple_of(slot * MOE_TILE, MOE_TILE), MOE_TILE), :] = y

            @pl.when(step < nfull_ref[0])
            def _():
                put_idx(idx).start()

            @pl.when(step == nfull_ref[0])
            def _():
                stage_wait((step + 1) % N_STAGES)
                stage_wait(prev)

        @pl.when(kind_ref[step] == KIND_PARTIAL)
        def _():
            y, idx = experts(lambda chunk: None)
            buf_ref[pl.ds(0, MOE_TILE), :] = y
            cp = put_idx(idx)
            cp.start()
            cp.wait()
            n = nvalid_ref[step]

            def one(r, carry):
                stage_copy(0, r).start()
                return carry

            def one_wait(r, carry):
                _row_copy(buf_ref, 0, out_ref, 0, scat_sem.at[0]).wait()
                return carry

            lax.fori_loop(0, n, one, 0)
            lax.fori_loop(0, n, one_wait, 0)

    return body


def _moe_call(plan, xs, p, layer):
    cfg = _cfg()

    def xmap(i, blk, *_):
        return (blk[i], 0)

    def wa(i, blk, ea, *_):
        return (layer, ea[i], 0, 0)

    def wb(i, blk, ea, eb, *_):
        return (layer, eb[i], 0, 0)

    def lmap(i, *_):
        return (layer, 0, 0)

    return pl.pallas_call(
        _make_moe_kernel(cfg),
        grid_spec=pltpu.PrefetchScalarGridSpec(
            num_scalar_prefetch=6,
            grid=(cfg.n_moe_steps,),
            in_specs=[
                pl.BlockSpec((MOE_TILE, ROW_W), xmap),
                pl.BlockSpec((1, 1, D_MODEL, 2 * D_EXPERT), wa),
                pl.BlockSpec((1, 1, D_EXPERT, D_MODEL), wa),
                pl.BlockSpec((1, 1, D_MODEL, 2 * D_EXPERT), wb),
                pl.BlockSpec((1, 1, D_EXPERT, D_MODEL), wb),
                pl.BlockSpec((1, 1, D_MODEL), lmap),
                pl.BlockSpec((1, 1, D_MODEL), lmap),
            ],
            out_specs=pl.BlockSpec(memory_space=pl.ANY),
            scratch_shapes=[
                pltpu.VMEM(((N_STAGES + 2) * MOE_TILE, D_MODEL), F32),
                pltpu.VMEM((8, MOE_TILE), I32),
                pltpu.SMEM((8, MOE_TILE), I32),
                pltpu.SemaphoreType.DMA((N_STAGES,)),
                pltpu.SemaphoreType.DMA(()),
            ],
        ),
        out_shape=jax.ShapeDtypeStruct((cfg.n_tokens, D_MODEL), F32),
        compiler_params=pltpu.CompilerParams(dimension_semantics=("arbitrary",), vmem_limit_bytes=VMEM_LIMIT),
        name="moe_experts",
    )(*plan, xs, p["w_gu"], p["w_down"], p["w_gu"], p["w_down"], p["ln2_g"], p["ln2_b"])


def _moe_plan(cnt_out, cur_out, nxt_out, tbl_out):
    cfg = _cfg()
    cnt = cnt_out[:N_CLASSES, 0].astype(I32)
    cur = cur_out[:N_CLASSES, 0].astype(I32)
    n_used = nxt_out[0, 0].astype(I32)
    tile_cls = jnp.clip(tbl_out[0, :cfg.n_pool_tiles].astype(I32), 0, N_CLASSES - 1)
    tid = jnp.arange(cfg.n_pool_tiles, dtype=I32)
    fill = cnt % MOE_TILE
    ends_here = (cur[None, :] == tid[:, None]) & (fill[None, :] != 0)
    is_partial = jnp.any(ends_here, axis=1)
    n_valid = jnp.where(is_partial, jnp.sum(jnp.where(ends_here, fill[None, :], 0), axis=1), MOE_TILE)
    used = tid < n_used
    big = jnp.int32(1 << 30)
    key = tile_cls * (2 * cfg.n_pool_tiles) + tid
    full = used & ~is_partial
    part = used & is_partial
    def order_of(keys):
        rank = jnp.sum(keys[None, :] < keys[:, None], axis=1)
        return jnp.sum(jnp.where(rank[None, :] == tid[:, None], tid[None, :], 0), axis=1).astype(I32)

    order_full = order_of(jnp.where(full, key, big + tid))
    order_part = order_of(jnp.where(part, key, big + tid))
    n_full = jnp.sum(full).astype(I32)
    n_part = jnp.sum(part).astype(I32)
    last_full = order_full[n_full - 1]

    i = jnp.arange(cfg.n_full_max + 1, dtype=I32)
    blk_a = order_full[jnp.minimum(i, n_full - 1)]
    kind_a = jnp.where(i <= n_full, KIND_FULL, KIND_IDLE)
    k = jnp.arange(N_CLASSES, dtype=I32)
    blk_b = jnp.where(n_part > 0, order_part[jnp.clip(jnp.minimum(k, n_part - 1), 0, None)], last_full)
    kind_b = jnp.where(k < n_part, KIND_PARTIAL, KIND_IDLE)
    blk = jnp.concatenate([blk_a, blk_b]).astype(I32)
    kind = jnp.concatenate([kind_a, kind_b]).astype(I32)
    cls = tile_cls[blk]
    base = (cls // N_PAIRS) * GROUP_SIZE
    ea = base + jnp.asarray(PAIR_LO, I32)[cls % N_PAIRS]
    eb = base + jnp.asarray(PAIR_HI, I32)[cls % N_PAIRS]
    return blk, ea.astype(I32), eb.astype(I32), kind, n_valid[blk].astype(I32), n_full.reshape(1)


def _router_rows(router_w):
    wt = router_w.astype(F32).T.reshape(N_GROUPS, GROUP_SIZE, D_MODEL)
    wt = jnp.transpose(wt, (1, 0, 2))
    wt = jnp.pad(wt, ((0, 0), (0, 8 - N_GROUPS), (0, 0))).reshape(32, D_MODEL)
    hi = wt.astype(BF16)
    lo = (wt - hi.astype(F32)).astype(BF16)
    return jnp.concatenate([hi, lo], axis=0)


def _router_bias_rows(router_b):
    b = jnp.transpose(router_b.astype(F32).reshape(N_GROUPS, GROUP_SIZE), (1, 0))
    return jnp.pad(b, ((0, 0), (0, 8 - N_GROUPS))).reshape(32, 1)


def kernel(x, mem, w_in_a, sgu_ln_g, sgu_ln_b, sgu_w, sgu_b, w_in_b, conv_w, w_kv, w_out, ln1_g, ln1_b,
           router_w, router_b, w_gate, w_up, w_down, ln2_g, ln2_b):
    cfg = _cfg()
    kt, v = _kv_call(mem, w_kv.astype(BF16))
    p = {
        "w_in_a": w_in_a.astype(BF16),
        "sgu_ln_g": sgu_ln_g[:, None, :],
        "sgu_ln_b": sgu_ln_b[:, None, :],
        "sgu_w": sgu_w,
        "sgu_b_col": jnp.broadcast_to(sgu_b[..., None], sgu_b.shape + (SGU_HEAD_DIM,)),
        "w_in_b": w_in_b.astype(BF16),
        "conv_w": conv_w,
        "kt": kt,
        "v": v,
        "w_out": w_out.astype(BF16),
        "ln1_g": ln1_g[:, None, :],
        "ln1_b": ln1_b[:, None, :],
        "rw": _router_rows(router_w),
        "rb": _router_bias_rows(router_b),
        "w_gu": jnp.concatenate([w_gate, w_up], axis=-1).astype(BF16),
        "w_down": w_down.astype(BF16),
        "ln2_g": ln2_g[:, None, :],
        "ln2_b": ln2_b[:, None, :],
    }
    h = x.reshape(cfg.n_tokens, D_MODEL)
    for layer in range(DEPTH):
        mixer = "gmlp" if layer % 2 == 0 else "conv"
        xs, cnt, cur, nxt, tbl = _mixer_call(mixer, h, p, layer, layer // 2)
        h = _moe_call(_moe_plan(cnt, cur, nxt, tbl), xs, p, layer)
    return h.reshape(BATCH, SEQ, D_MODEL)
```

```python
import types

import jax
import jax.numpy as jnp
from jax import lax
from jax.experimental import pallas as pl
from jax.experimental.pallas import tpu as pltpu

D_MODEL = 1024
BATCH = 16
SEQ = 4096
DEPTH = 2
CHUNK = 64
MIX_WIDTH = D_MODEL // 2
SGU_CHUNK = 128
SGU_HEADS = 4
SGU_HEAD_DIM = MIX_WIDTH // SGU_HEADS
CONV_WIDTH = 3
N_MEM = 256
XA_HEADS = 4
XA_HEAD_DIM = MIX_WIDTH // XA_HEADS
N_EXPERTS = 16
N_GROUPS = 4
GROUP_SIZE = N_EXPERTS // N_GROUPS
D_EXPERT = D_MODEL // 2
ALPHA = (2 * DEPTH) ** 0.25
LN_EPS = 1e-5

N_PAIRS = GROUP_SIZE * (GROUP_SIZE - 1) // 2
N_CLASSES = N_GROUPS * N_PAIRS
CLASS_ROWS = 32
PAIR_LO = (0, 0, 0, 1, 1, 2)
PAIR_HI = (1, 2, 3, 2, 3, 3)

LANES = 128
TM = 512
MOE_TILE = 256
ROW_W = D_MODEL + LANES
META_IDX, META_GA, META_GB = 0, 1, 2
HALO = 8
MIXER_STAGES = 4
N_STAGES = 3
MIXER_SLAB = 256
MIXER_ISSUE_ROWS = 32
MOE_SLAB = 256
VMEM_LIMIT = 56 * 1024 * 1024
KIND_FULL, KIND_PARTIAL, KIND_IDLE = 0, 1, 2

F32 = jnp.float32
BF16 = jnp.bfloat16
I32 = jnp.int32


def _cfg():
    n_tokens = BATCH * SEQ
    n_full_max = n_tokens // MOE_TILE
    n_pool_tiles = n_full_max + N_CLASSES
    assert n_pool_tiles <= TM and SEQ % TM == 0 and TM % SGU_CHUNK == 0
    assert n_tokens // TM >= MIXER_STAGES and n_tokens - N_CLASSES * (MOE_TILE - 1) >= N_STAGES * MOE_TILE
    return types.SimpleNamespace(
        n_tokens=n_tokens, n_mixer_steps=n_tokens // TM, tiles_per_seq=SEQ // TM,
        n_full_max=n_full_max, n_pool_tiles=n_pool_tiles, n_pool_rows=n_pool_tiles * MOE_TILE,
        n_moe_steps=n_full_max + 1 + N_CLASSES)


def _dot(a, b):
    return jnp.dot(a, b, preferred_element_type=F32)


def _dot_nt(a, b):
    return lax.dot_general(a, b, (((1,), (1,)), ((), ())), preferred_element_type=F32)


def _layer_norm(x, g, b):
    mu = jnp.mean(x, axis=-1, keepdims=True)
    xc = x - mu
    var = jnp.mean(xc * xc, axis=-1, keepdims=True)
    return xc * lax.rsqrt(var + LN_EPS) * g + b


def _sigmoid(x):
    return 1.0 / (1.0 + jnp.exp(-x))


def _row_copy(src_ref, src_row, dst_ref, dst_row, sem):
    return pltpu.make_async_copy(src_ref.at[pl.ds(src_row, 1)], dst_ref.at[pl.ds(dst_row, 1)], sem)


def _kv_kernel(mem_ref, wkv_ref, kt_ref, v_ref):
    kv = _dot(mem_ref[0].astype(BF16), wkv_ref[0])
    kt_ref[0, 0] = kv[:, :MIX_WIDTH].T.astype(BF16)
    v_ref[0, 0] = kv[:, MIX_WIDTH:].astype(BF16)


def _kv_call(mem, wkv_bf16):
    return pl.pallas_call(
        _kv_kernel,
        grid=(DEPTH, BATCH),
        in_specs=[
            pl.BlockSpec((1, N_MEM, D_MODEL), lambda l, b: (b, 0, 0)),
            pl.BlockSpec((1, D_MODEL, 2 * MIX_WIDTH), lambda l, b: (l, 0, 0)),
        ],
        out_specs=[
            pl.BlockSpec((1, 1, MIX_WIDTH, N_MEM), lambda l, b: (l, b, 0, 0)),
            pl.BlockSpec((1, 1, N_MEM, MIX_WIDTH), lambda l, b: (l, b, 0, 0)),
        ],
        out_shape=[
            jax.ShapeDtypeStruct((DEPTH, BATCH, MIX_WIDTH, N_MEM), BF16),
            jax.ShapeDtypeStruct((DEPTH, BATCH, N_MEM, MIX_WIDTH), BF16),
        ],
        name="kv_proj",
    )(mem, wkv_bf16)


def _route(y, step, rw_ref, rb_ref, tri_ref, low_ref, cnt_ref, cur_ref, nxt_ref, tbl_ref):
    y_hi = y.astype(BF16)
    y_lo = (y - y_hi.astype(F32)).astype(BF16)
    l_hi = _dot_nt(rw_ref[...], y_hi)
    l_lo = _dot_nt(rw_ref[0:32, :], y_lo)
    logits = l_hi[0:32] + l_hi[32:64] + l_lo
    scores = _sigmoid(logits)
    sel = scores + rb_ref[...]

    row = lax.broadcasted_iota(I32, (8, TM), 0).astype(F32)
    s = [sel[8 * m:8 * m + 8] for m in range(GROUP_SIZE)]
    sc = [scores[8 * m:8 * m + 8] for m in range(GROUP_SIZE)]
    neg = jnp.float32(-jnp.inf)
    m1 = jnp.maximum(jnp.maximum(s[0], s[1]), jnp.maximum(s[2], s[3]))
    i1 = jnp.where(s[0] == m1, 0.0, jnp.where(s[1] == m1, 1.0, jnp.where(s[2] == m1, 2.0, 3.0)))
    t = [jnp.where(i1 == float(m), neg, s[m]) for m in range(GROUP_SIZE)]
    m2 = jnp.maximum(jnp.maximum(t[0], t[1]), jnp.maximum(t[2], t[3]))
    i2 = jnp.where(t[0] == m2, 0.0, jnp.where(t[1] == m2, 1.0, jnp.where(t[2] == m2, 2.0, 3.0)))
    gscore = jnp.where(row < float(N_GROUPS), m1 + m2, neg)
    gmax = jnp.max(gscore, axis=0, keepdims=True)
    top_group = jnp.min(jnp.where(gscore == gmax, row, 8.0), axis=0, keepdims=True)
    lo = jnp.minimum(i1, i2)
    hi = jnp.maximum(i1, i2)
    pair = jnp.where(lo == 0.0, 0.0, jnp.where(lo == 1.0, 3.0, 5.0)) + hi - lo - 1.0
    s_lo = sum(jnp.where(lo == float(m), sc[m], 0.0) for m in range(GROUP_SIZE))
    s_hi = sum(jnp.where(hi == float(m), sc[m], 0.0) for m in range(GROUP_SIZE))
    inv = 1.0 / (s_lo + s_hi)
    chosen = row == top_group
    cls = jnp.sum(jnp.where(chosen, row * float(N_PAIRS) + pair, 0.0), axis=0, keepdims=True)
    ga = jnp.sum(jnp.where(chosen, s_lo * inv, 0.0), axis=0, keepdims=True)
    gb = jnp.sum(jnp.where(chosen, s_hi * inv, 0.0), axis=0, keepdims=True)

    crow = lax.broadcasted_iota(I32, (CLASS_ROWS, TM), 0).astype(F32)
    onehot = jnp.where(crow == cls, 1.0, 0.0)
    before = _dot(onehot.astype(BF16), tri_ref[...])
    cnt = cnt_ref[...]
    cur = cur_ref[...]
    nxt = nxt_ref[0:1, :]
    tile_f = float(MOE_TILE)
    inv_tile = 1.0 / MOE_TILE
    rank = cnt + before
    cnt_new = cnt + jnp.sum(onehot, axis=1, keepdims=True)
    tiles_old = jnp.floor((cnt + (tile_f - 1.0)) * inv_tile)
    tiles_new = jnp.floor((cnt_new + (tile_f - 1.0)) * inv_tile)
    fresh = tiles_new - tiles_old
    base = nxt + _dot(low_ref[...], fresh.astype(BF16))
    q = jnp.floor(rank * inv_tile)
    tile_of = jnp.where(q < tiles_old, cur, base + (q - tiles_old))
    pos_all = tile_of * tile_f + (rank - q * tile_f)
    pos = jnp.sum(onehot * pos_all, axis=0, keepdims=True)

    cnt_ref[...] = cnt_new
    cur_ref[...] = jnp.where(fresh > 0.0, base + fresh - 1.0, cur)
    nxt_ref[...] = jnp.broadcast_to(nxt + jnp.sum(fresh, axis=0, keepdims=True), nxt_ref.shape)
    ids = lax.broadcasted_iota(I32, (CLASS_ROWS, TM), 1).astype(F32)
    hit = jnp.where((ids >= base) & (ids < base + fresh), 1.0, 0.0)
    cls_hit = jnp.sum(hit * crow, axis=0, keepdims=True)
    any_hit = jnp.sum(hit, axis=0, keepdims=True)
    tbl_ref[...] = jnp.where(any_hit > 0.0, cls_hit, tbl_ref[...])

    idx = (lax.broadcasted_iota(I32, (1, TM), 1) + step * TM).astype(F32)
    return pos, idx, ga, gb


def _make_mixer_kernel(mixer, cfg):
    def body(*refs):
        if mixer == "gmlp":
            (x_ref, win_ref, lng_ref, lnb_ref, sw_ref, sb_ref, *rest) = refs
        else:
            (x_ref, xh_ref, win_ref, cw_ref, *rest) = refs
        (kt_ref, v_ref, wout_ref, g1_ref, b1_ref, rw_ref, rb_ref,
         xs_ref, cnt_out, cur_out, nxt_out, tbl_out,
         tri_ref, low_ref, cnt_ref, cur_ref, nxt_ref, tbl_ref, buf_ref, posv_ref, poss_ref,
         stv_ref, sts_ref, scat_sem, pos_sem, st_sem) = rest

        step = pl.program_id(0)
        last = pl.num_programs(0) - 1
        slot = step % MIXER_STAGES
        out_stage = (step + MIXER_STAGES - 2) % MIXER_STAGES
        pbuf = step % 2
        zbuf = buf_ref.at[pl.ds(MIXER_STAGES * TM, TM)]
        catbuf = buf_ref.at[pl.ds((MIXER_STAGES + 1) * TM, TM), pl.ds(0, D_MODEL)]
        obuf = buf_ref.at[pl.ds((MIXER_STAGES + 1) * TM, TM), pl.ds(D_MODEL, D_MODEL)]

        def stage_copy(stage, r, b):
            src = buf_ref.at[pl.ds(stage * TM + r, 1), pl.ds(0, ROW_W)]
            return pltpu.make_async_copy(src, xs_ref.at[pl.ds(poss_ref[b, r], 1)], scat_sem.at[stage])

        def pos_copy(b):
            return pltpu.make_async_copy(posv_ref.at[pl.ds(b, 1)], poss_ref.at[pl.ds(b, 1)], pos_sem.at[b])

        def stage_wait(stage):
            pltpu.make_async_copy(buf_ref.at[pl.ds(0, TM), pl.ds(0, ROW_W)], xs_ref.at[pl.ds(0, TM)],
                                  scat_sem.at[stage]).wait()

        @pl.when(step == 0)
        def _():
            r = lax.broadcasted_iota(I32, (TM, TM), 0)
            c = lax.broadcasted_iota(I32, (TM, TM), 1)
            tri_ref[...] = jnp.where(r < c, 1.0, 0.0).astype(BF16)
            r = lax.broadcasted_iota(I32, (CLASS_ROWS, CLASS_ROWS), 0)
            c = lax.broadcasted_iota(I32, (CLASS_ROWS, CLASS_ROWS), 1)
            low_ref[...] = jnp.where(c < r, 1.0, 0.0).astype(BF16)
            cnt_ref[...] = jnp.zeros_like(cnt_ref)
            cur_ref[...] = jnp.full_like(cur_ref, -1.0)
            nxt_ref[...] = jnp.zeros_like(nxt_ref)
            tbl_ref[...] = jnp.full_like(tbl_ref, -1.0)
            buf_ref[pl.ds((MIXER_STAGES - 2) * TM, 2 * TM), 0:ROW_W] = jnp.zeros((2 * TM, ROW_W), F32)

            def init_pos(r, carry):
                poss_ref[0, r] = r
                poss_ref[1, r] = r
                return carry

            lax.fori_loop(0, TM, init_pos, 0)

        @pl.when(step >= 2)
        def _():
            pos_copy(pbuf).wait()

        issued = [0]

        def issue_next():
            lo = issued[0]
            hi = min(lo + MIXER_ISSUE_ROWS, TM)
            for r in range(lo, hi):
                stage_copy(out_stage, r, pbuf).start()
            issued[0] = hi

        x = x_ref[...]
        xb = x.astype(BF16)
        for n in range(win_ref.shape[-1] // MIXER_SLAB):
            cols = slice(n * MIXER_SLAB, (n + 1) * MIXER_SLAB)
            issue_next()
            zbuf[:, cols] = _dot(xb, win_ref[0, :, cols])
        if mixer == "gmlp":
            zuv = zbuf[:, 0:2 * MIX_WIDTH]
            uv = zuv * (lax.erf(zuv * (2.0 ** -0.5)) + 1.0) * 0.5
            u = uv[:, :MIX_WIDTH]
            vn = _layer_norm(uv[:, MIX_WIDTH:], lng_ref[0], lnb_ref[0]).astype(BF16)
            pos_r = lax.broadcasted_iota(I32, (SGU_CHUNK, SGU_CHUNK), 0) // CHUNK
            pos_c = lax.broadcasted_iota(I32, (SGU_CHUNK, SGU_CHUNK), 1) // CHUNK
            causal = pos_r >= pos_c
            for c in range(TM // SGU_CHUNK):
                rows = slice(c * SGU_CHUNK, (c + 1) * SGU_CHUNK)
                blocks = []
                for h in range(SGU_HEADS):
                    w = jnp.where(causal, sw_ref[0, h], 0.0).astype(BF16)
                    blk = vn[rows, h * SGU_HEAD_DIM:(h + 1) * SGU_HEAD_DIM]
                    blocks.append(_dot(w, blk) + sb_ref[0, h])
                issue_next()
                catbuf[rows, 0:MIX_WIDTH] = u[rows] * jnp.concatenate(blocks, axis=1)
            q = zbuf[:, 2 * MIX_WIDTH:3 * MIX_WIDTH]
        else:
            bg = zbuf[:, 0:MIX_WIDTH]
            hcur = zbuf[:, MIX_WIDTH:2 * MIX_WIDTH] * zbuf[:, 2 * MIX_WIDTH:3 * MIX_WIDTH]
            zh = _dot(xh_ref[...].astype(BF16), win_ref[0, :, MIX_WIDTH:3 * MIX_WIDTH])
            first_of_seq = (step % cfg.tiles_per_seq) == 0
            hprev = jnp.where(first_of_seq, 0.0, zh[:, :MIX_WIDTH] * zh[:, MIX_WIDTH:])
            rowid = lax.broadcasted_iota(I32, (TM, MIX_WIDTH), 0)
            h1 = jnp.where(rowid == 0, hprev[HALO - 1:HALO], pltpu.roll(hcur, 1, 0))
            h2 = jnp.where(rowid == 0, hprev[HALO - 2:HALO - 1],
                           jnp.where(rowid == 1, hprev[HALO - 1:HALO], pltpu.roll(hcur, 2, 0)))
            cw = cw_ref[0]
            issue_next()
            catbuf[:, 0:MIX_WIDTH] = bg * (cw[0:1] * h2 + cw[1:2] * h1 + cw[2:3] * hcur)
            q = zbuf[:, 3 * MIX_WIDTH:4 * MIX_WIDTH]

        scale = XA_HEAD_DIM ** -0.5
        for h in range(XA_HEADS):
            lo, hi = h * XA_HEAD_DIM, (h + 1) * XA_HEAD_DIM
            sc = _dot(q[:, lo:hi].astype(BF16), kt_ref[0, 0, lo:hi, :]) * scale
            e = jnp.exp(sc - jnp.max(sc, axis=-1, keepdims=True))
            p = e * (1.0 / jnp.sum(e, axis=-1, keepdims=True))
            issue_next()
            catbuf[:, MIX_WIDTH + lo:MIX_WIDTH + hi] = _dot(p.astype(BF16), v_ref[0, 0, :, lo:hi])
        cat = catbuf[:, 0:D_MODEL].astype(BF16)
        for n in range(D_MODEL // MIXER_SLAB):
            cols = slice(n * MIXER_SLAB, (n + 1) * MIXER_SLAB)
            issue_next()
            obuf[:, cols] = _dot(cat, wout_ref[0, :, cols])
        while issued[0] < TM:
            issue_next()
        y = _layer_norm(ALPHA * x + obuf[...], g1_ref[0], b1_ref[0])

        pos, idx, ga, gb = _route(y, step, rw_ref, rb_ref, tri_ref, low_ref, cnt_ref, cur_ref, nxt_ref, tbl_ref)
        mrow = lax.broadcasted_iota(I32, (LANES, TM), 0)
        meta = jnp.where(mrow == META_IDX, idx,
                         jnp.where(mrow == META_GA, ga, jnp.where(mrow == META_GB, gb, 0.0)))
        meta_t = jnp.transpose(meta)
        posv_ref[pl.ds(pbuf, 1), :] = pos.astype(I32)
        cnt_out[...] = cnt_ref[:, 0:LANES]
        cur_out[...] = cur_ref[:, 0:LANES]
        nxt_out[...] = nxt_ref[:, 0:LANES]
        tbl_out[...] = tbl_ref[...]

        @pl.when(step < 2)
        def _():
            stage_wait(out_stage)

        @pl.when(step >= MIXER_STAGES)
        def _():
            stage_wait(slot)

        own = pl.ds(pl.multiple_of(slot * TM, TM), TM)
        buf_ref[own, 0:D_MODEL] = y
        buf_ref[own, D_MODEL:ROW_W] = meta_t
        pos_copy(pbuf).start()

        @pl.when(step == last)
        def _():
            pos_copy(0).wait()
            pos_copy(1).wait()
            prev_stage = (step + MIXER_STAGES - 1) % MIXER_STAGES

            def issue_last(r, carry):
                stage_copy(prev_stage, r, 1 - pbuf).start()
                stage_copy(slot, r, pbuf).start()
                return carry

            lax.fori_loop(0, TM, issue_last, 0, unroll=8)
            lane = lax.broadcasted_iota(I32, (CLASS_ROWS, LANES), 1).astype(F32)
            crow = lax.broadcasted_iota(I32, (CLASS_ROWS, LANES), 0).astype(F32)
            diag = crow == lane
            cnt_l = jnp.sum(jnp.where(diag, cnt_ref[:, 0:LANES], 0.0), axis=0, keepdims=True)
            cur_l = jnp.sum(jnp.where(diag, cur_ref[:, 0:LANES], 0.0), axis=0, keepdims=True)
            srow = lax.broadcasted_iota(I32, (8, LANES), 0)
            st = jnp.where(srow == 0, cnt_l, jnp.where(srow == 1, cur_l, nxt_ref[:, 0:LANES]))
            stv_ref[...] = st.astype(I32)
            cp = pltpu.make_async_copy(stv_ref, sts_ref, st_sem)
            cp.start()
            cp.wait()
            for stage in range(MIXER_STAGES):
                stage_wait(stage)
            zero_row = pl.multiple_of(out_stage * TM, TM)
            buf_ref[pl.ds(zero_row, TM), 0:ROW_W] = jnp.zeros((TM, ROW_W), F32)
            zrow = buf_ref.at[pl.ds(zero_row, MOE_TILE), pl.ds(0, ROW_W)]
            zsem = scat_sem.at[0]

            def clear_class(c, carry):
                fill = sts_ref[0, c] % MOE_TILE
                start = sts_ref[1, c] * MOE_TILE + fill
                n = jnp.where(fill == 0, 0, MOE_TILE - fill)

                def one(j, carry2):
                    _row_copy(zrow, 0, xs_ref, start + j, zsem).start()
                    return carry2

                def one_wait(j, carry2):
                    _row_copy(zrow, 0, xs_ref, 0, zsem).wait()
                    return carry2

                lax.fori_loop(0, n, one, 0)
                lax.fori_loop(0, n, one_wait, 0)
                return carry

            lax.fori_loop(0, N_CLASSES, clear_class, 0)

            def clear_tile(t, carry):
                start = pl.multiple_of(t * MOE_TILE, MOE_TILE)
                cp2 = pltpu.make_async_copy(zrow, xs_ref.at[pl.ds(start, MOE_TILE)], zsem)
                cp2.start()
                cp2.wait()
                return carry

            lax.fori_loop(sts_ref[2, 0], cfg.n_pool_tiles, clear_tile, 0)

    return body


def _mixer_call(mixer, x, p, layer, j):
    cfg = _cfg()

    def const_spec(shape):
        return pl.BlockSpec(shape, lambda i: (0,) * len(shape))

    def layer_spec(shape, l):
        return pl.BlockSpec((1,) + shape, lambda i: (l,) + (0,) * len(shape))

    x_spec = pl.BlockSpec((TM, D_MODEL), lambda i: (i, 0))
    if mixer == "gmlp":
        head_specs = [
            x_spec,
            layer_spec((D_MODEL, 3 * MIX_WIDTH), j),
            layer_spec((1, MIX_WIDTH), j),
            layer_spec((1, MIX_WIDTH), j),
            layer_spec((SGU_HEADS, SGU_CHUNK, SGU_CHUNK), j),
            layer_spec((SGU_HEADS, SGU_CHUNK, SGU_HEAD_DIM), j),
        ]
        head_args = (x, p["w_in_a"], p["sgu_ln_g"], p["sgu_ln_b"], p["sgu_w"], p["sgu_b_col"])
    else:
        head_specs = [
            x_spec,
            pl.BlockSpec((HALO, D_MODEL), lambda i: (jnp.maximum(i * (TM // HALO) - 1, 0), 0)),
            layer_spec((D_MODEL, 4 * MIX_WIDTH), j),
            layer_spec((CONV_WIDTH, MIX_WIDTH), j),
        ]
        head_args = (x, x, p["w_in_b"], p["conv_w"])
    tail_specs = [
        pl.BlockSpec((1, 1, MIX_WIDTH, N_MEM), lambda i: (layer, i // cfg.tiles_per_seq, 0, 0)),
        pl.BlockSpec((1, 1, N_MEM, MIX_WIDTH), lambda i: (layer, i // cfg.tiles_per_seq, 0, 0)),
        layer_spec((D_MODEL, D_MODEL), layer),
        layer_spec((1, D_MODEL), layer),
        layer_spec((1, D_MODEL), layer),
        const_spec((64, D_MODEL)),
        const_spec((32, 1)),
    ]
    tail_args = (p["kt"], p["v"], p["w_out"], p["ln1_g"], p["ln1_b"], p["rw"], p["rb"])
    return pl.pallas_call(
        _make_mixer_kernel(mixer, cfg),
        grid=(cfg.n_mixer_steps,),
        in_specs=head_specs + tail_specs,
        out_specs=[
            pl.BlockSpec(memory_space=pl.ANY),
            const_spec((CLASS_ROWS, LANES)),
            const_spec((CLASS_ROWS, LANES)),
            const_spec((8, LANES)),
            const_spec((8, TM)),
        ],
        out_shape=[
            jax.ShapeDtypeStruct((cfg.n_pool_rows, ROW_W), F32),
            jax.ShapeDtypeStruct((CLASS_ROWS, LANES), F32),
            jax.ShapeDtypeStruct((CLASS_ROWS, LANES), F32),
            jax.ShapeDtypeStruct((8, LANES), F32),
            jax.ShapeDtypeStruct((8, TM), F32),
        ],
        scratch_shapes=[
            pltpu.VMEM((TM, TM), BF16),
            pltpu.VMEM((CLASS_ROWS, CLASS_ROWS), BF16),
            pltpu.VMEM((CLASS_ROWS, TM), F32),
            pltpu.VMEM((CLASS_ROWS, TM), F32),
            pltpu.VMEM((8, TM), F32),
            pltpu.VMEM((8, TM), F32),
            pltpu.VMEM(((MIXER_STAGES + 2) * TM, 2 * D_MODEL), F32),
            pltpu.VMEM((2, TM), I32),
            pltpu.SMEM((2, TM), I32),
            pltpu.VMEM((8, LANES), I32),
            pltpu.SMEM((8, LANES), I32),
            pltpu.SemaphoreType.DMA((MIXER_STAGES,)),
            pltpu.SemaphoreType.DMA((2,)),
            pltpu.SemaphoreType.DMA(()),
        ],
        compiler_params=pltpu.CompilerParams(dimension_semantics=("arbitrary",), vmem_limit_bytes=VMEM_LIMIT),
        name="mixer_" + mixer,
    )(*head_args, *tail_args)


def _make_moe_kernel(cfg):
    n_slabs = D_MODEL // MOE_SLAB
    rows_per_chunk = MOE_TILE // (4 * n_slabs)
    work_row, facc_row = N_STAGES * MOE_TILE, (N_STAGES + 1) * MOE_TILE

    def body(blk_ref, ea_ref, eb_ref, kind_ref, nvalid_ref, nfull_ref,
             xs_ref, wgu_a_ref, wd_a_ref, wgu_b_ref, wd_b_ref, g2_ref, b2_ref, out_ref,
             buf_ref, idxv_ref, idxs_ref, scat_sem, idx_sem):
        step = pl.program_id(0)
        slot = step % N_STAGES
        prev = (step + N_STAGES - 1) % N_STAGES
        work = buf_ref.at[pl.ds(work_row, MOE_TILE)]
        facc = buf_ref.at[pl.ds(facc_row, MOE_TILE)]

        ibuf = step % 2
        pbuf = 1 - ibuf

        def stage_copy(stage, r, b):
            return _row_copy(buf_ref, stage * MOE_TILE + r, out_ref, idxs_ref[b * 8, r], scat_sem.at[stage])

        def stage_wait(stage):
            pltpu.make_async_copy(buf_ref.at[pl.ds(0, MOE_TILE)], out_ref.at[pl.ds(0, MOE_TILE)],
                                  scat_sem.at[stage]).wait()

        def experts(issue):
            x = xs_ref[:, 0:D_MODEL]
            xb = x.astype(BF16)
            meta = xs_ref[:, D_MODEL:ROW_W]
            gates = (meta[:, META_GA:META_GA + 1], meta[:, META_GB:META_GB + 1])
            chunk = 0
            for k, (wgu_ref, wd_ref) in enumerate(((wgu_a_ref, wd_a_ref), (wgu_b_ref, wd_b_ref))):
                for n in range(n_slabs):
                    cols = slice(n * MOE_SLAB, (n + 1) * MOE_SLAB)
                    issue(chunk)
                    chunk += 1
                    work[:, cols] = _dot(xb, wgu_ref[0, 0, :, cols])
                g = work[:, :D_EXPERT]
                hb = (g * _sigmoid(g) * work[:, D_EXPERT:]).astype(BF16)
                for n in range(n_slabs):
                    cols = slice(n * MOE_SLAB, (n + 1) * MOE_SLAB)
                    issue(chunk)
                    chunk += 1
                    part = gates[k] * _dot(hb, wd_ref[0, 0, :, cols])
                    facc[:, cols] = part if k == 0 else facc[:, cols] + part
            return _layer_norm(ALPHA * x + facc[...], g2_ref[0], b2_ref[0])

        def idx_copy(b):
            dst = idxs_ref.at[pl.ds(pl.multiple_of(b * 8, 8), 8)]
            return pltpu.make_async_copy(idxv_ref, dst, idx_sem.at[b])

        def put_idx(b):
            idx = jnp.transpose(xs_ref[:, D_MODEL:ROW_W])[META_IDX:META_IDX + 1, :]
            idxv_ref[...] = jnp.broadcast_to(idx, idxv_ref.shape).astype(I32)
            return idx_copy(b)

        @pl.when(step == 0)
        def _():
            buf_ref[pl.ds((N_STAGES - 1) * MOE_TILE, MOE_TILE), :] = jnp.zeros((MOE_TILE, D_MODEL), F32)

            def init_idx(r, carry):
                idxs_ref[8, r] = r
                return carry

            lax.fori_loop(0, MOE_TILE, init_idx, 0)

        @pl.when(kind_ref[step] == KIND_FULL)
        def _():
            @pl.when(step > 0)
            def _():
                idx_copy(pbuf).wait()

            @pl.when(step < nfull_ref[0])
            def _():
                put_idx(ibuf).start()

            def issue(chunk):
                for r in range(chunk * rows_per_chunk, (chunk + 1) * rows_per_chunk):
                    stage_copy(prev, r, pbuf).start()

            y = experts(issue)

            @pl.when(step == 0)
            def _():
                stage_wait(prev)

            @pl.when(step >= N_STAGES)
            def _():
                stage_wait(slot)

            buf_ref[pl.ds(pl.multiple_of(slot * MOE_TILE, MOE_TILE), MOE_TILE), :] = y

            @pl.when(step == nfull_ref[0])
            def _():
                stage_wait((step + 1) % N_STAGES)
                stage_wait(prev)

        @pl.when(kind_ref[step] == KIND_PARTIAL)
        def _():
            cp = put_idx(0)
            cp.start()
            buf_ref[pl.ds(0, MOE_TILE), :] = experts(lambda chunk: None)
            cp.wait()
            n = nvalid_ref[step]

            def one(r, carry):
                stage_copy(0, r, 0).start()
                return carry

            def one_wait(r, carry):
                _row_copy(buf_ref, 0, out_ref, 0, scat_sem.at[0]).wait()
                return carry

            lax.fori_loop(0, n, one, 0)
            lax.fori_loop(0, n, one_wait, 0)

    return body


def _moe_call(plan, xs, p, layer):
    cfg = _cfg()

    def xmap(i, blk, *_):
        return (blk[i], 0)

    def wa(i, blk, ea, *_):
        return (layer, ea[i], 0, 0)

    def wb(i, blk, ea, eb, *_):
        return (layer, eb[i], 0, 0)

    def lmap(i, *_):
        return (layer, 0, 0)

    return pl.pallas_call(
        _make_moe_kernel(cfg),
        grid_spec=pltpu.PrefetchScalarGridSpec(
            num_scalar_prefetch=6,
            grid=(cfg.n_moe_steps,),
            in_specs=[
                pl.BlockSpec((MOE_TILE, ROW_W), xmap),
                pl.BlockSpec((1, 1, D_MODEL, 2 * D_EXPERT), wa),
                pl.BlockSpec((1, 1, D_EXPERT, D_MODEL), wa),
                pl.BlockSpec((1, 1, D_MODEL, 2 * D_EXPERT), wb),
                pl.BlockSpec((1, 1, D_EXPERT, D_MODEL), wb),
                pl.BlockSpec((1, 1, D_MODEL), lmap),
                pl.BlockSpec((1, 1, D_MODEL), lmap),
            ],
            out_specs=pl.BlockSpec(memory_space=pl.ANY),
            scratch_shapes=[
                pltpu.VMEM(((N_STAGES + 2) * MOE_TILE, D_MODEL), F32),
                pltpu.VMEM((8, MOE_TILE), I32),
                pltpu.SMEM((16, MOE_TILE), I32),
                pltpu.SemaphoreType.DMA((N_STAGES,)),
                pltpu.SemaphoreType.DMA((2,)),
            ],
        ),
        out_shape=jax.ShapeDtypeStruct((cfg.n_tokens, D_MODEL), F32),
        compiler_params=pltpu.CompilerParams(dimension_semantics=("arbitrary",), vmem_limit_bytes=VMEM_LIMIT),
        name="moe_experts",
    )(*plan, xs, p["w_gu"], p["w_down"], p["w_gu"], p["w_down"], p["ln2_g"], p["ln2_b"])


def _moe_plan(cnt_out, cur_out, nxt_out, tbl_out):
    cfg = _cfg()
    cnt = cnt_out[:N_CLASSES, 0].astype(I32)
    cur = cur_out[:N_CLASSES, 0].astype(I32)
    n_used = nxt_out[0, 0].astype(I32)
    tile_cls = jnp.clip(tbl_out[0, :cfg.n_pool_tiles].astype(I32), 0, N_CLASSES - 1)
    tid = jnp.arange(cfg.n_pool_tiles, dtype=I32)
    fill = cnt % MOE_TILE
    ends_here = (cur[None, :] == tid[:, None]) & (fill[None, :] != 0)
    is_partial = jnp.any(ends_here, axis=1)
    n_valid = jnp.where(is_partial, jnp.sum(jnp.where(ends_here, fill[None, :], 0), axis=1), MOE_TILE)
    used = tid < n_used
    big = jnp.int32(1 << 30)
    key = tile_cls * (2 * cfg.n_pool_tiles) + tid
    full = used & ~is_partial
    part = used & is_partial
    def order_of(keys):
        rank = jnp.sum(keys[None, :] < keys[:, None], axis=1)
        return jnp.sum(jnp.where(rank[None, :] == tid[:, None], tid[None, :], 0), axis=1).astype(I32)

    order_full = order_of(jnp.where(full, key, big + tid))
    order_part = order_of(jnp.where(part, key, big + tid))
    n_full = jnp.sum(full).astype(I32)
    n_part = jnp.sum(part).astype(I32)
    last_full = order_full[n_full - 1]

    i = jnp.arange(cfg.n_full_max + 1, dtype=I32)
    blk_a = order_full[jnp.minimum(i, n_full - 1)]
    kind_a = jnp.where(i <= n_full, KIND_FULL, KIND_IDLE)
    k = jnp.arange(N_CLASSES, dtype=I32)
    blk_b = jnp.where(n_part > 0, order_part[jnp.clip(jnp.minimum(k, n_part - 1), 0, None)], last_full)
    kind_b = jnp.where(k < n_part, KIND_PARTIAL, KIND_IDLE)
    blk = jnp.concatenate([blk_a, blk_b]).astype(I32)
    kind = jnp.concatenate([kind_a, kind_b]).astype(I32)
    cls = tile_cls[blk]
    base = (cls // N_PAIRS) * GROUP_SIZE
    ea = base + jnp.asarray(PAIR_LO, I32)[cls % N_PAIRS]
    eb = base + jnp.asarray(PAIR_HI, I32)[cls % N_PAIRS]
    return blk, ea.astype(I32), eb.astype(I32), kind, n_valid[blk].astype(I32), n_full.reshape(1)


def _router_rows(router_w):
    wt = router_w.astype(F32).T.reshape(N_GROUPS, GROUP_SIZE, D_MODEL)
    wt = jnp.transpose(wt, (1, 0, 2))
    wt = jnp.pad(wt, ((0, 0), (0, 8 - N_GROUPS), (0, 0))).reshape(32, D_MODEL)
    hi = wt.astype(BF16)
    lo = (wt - hi.astype(F32)).astype(BF16)
    return jnp.concatenate([hi, lo], axis=0)


def _router_bias_rows(router_b):
    b = jnp.transpose(router_b.astype(F32).reshape(N_GROUPS, GROUP_SIZE), (1, 0))
    return jnp.pad(b, ((0, 0), (0, 8 - N_GROUPS))).reshape(32, 1)


def kernel(x, mem, w_in_a, sgu_ln_g, sgu_ln_b, sgu_w, sgu_b, w_in_b, conv_w, w_kv, w_out, ln1_g, ln1_b,
           router_w, router_b, w_gate, w_up, w_down, ln2_g, ln2_b):
    cfg = _cfg()
    kt, v = _kv_call(mem, w_kv.astype(BF16))
    p = {
        "w_in_a": w_in_a.astype(BF16),
        "sgu_ln_g": sgu_ln_g[:, None, :],
        "sgu_ln_b": sgu_ln_b[:, None, :],
        "sgu_w": sgu_w,
        "sgu_b_col": jnp.broadcast_to(sgu_b[..., None], sgu_b.shape + (SGU_HEAD_DIM,)),
        "w_in_b": w_in_b.astype(BF16),
        "conv_w": conv_w,
        "kt": kt,
        "v": v,
        "w_out": w_out.astype(BF16),
        "ln1_g": ln1_g[:, None, :],
        "ln1_b": ln1_b[:, None, :],
        "rw": _router_rows(router_w),
        "rb": _router_bias_rows(router_b),
        "w_gu": jnp.concatenate([w_gate, w_up], axis=-1).astype(BF16),
        "w_down": w_down.astype(BF16),
        "ln2_g": ln2_g[:, None, :],
        "ln2_b": ln2_b[:, None, :],
    }
    h = x.reshape(cfg.n_tokens, D_MODEL)
    for layer in range(DEPTH):
        mixer = "gmlp" if layer % 2 == 0 else "conv"
        xs, cnt, cur, nxt, tbl = _mixer_call(mixer, h, p, layer, layer // 2)
        h = _moe_call(_moe_plan(cnt, cur, nxt, tbl), xs, p, layer)
    return h.reshape(BATCH, SEQ, D_MODEL)
```

```python
import types

import jax
import jax.numpy as jnp
from jax import lax
from jax.experimental import pallas as pl
from jax.experimental.pallas import tpu as pltpu

D_MODEL = 1024
BATCH = 16
SEQ = 4096
DEPTH = 2
CHUNK = 64
MIX_WIDTH = D_MODEL // 2
SGU_CHUNK = 128
SGU_HEADS = 4
SGU_HEAD_DIM = MIX_WIDTH // SGU_HEADS
CONV_WIDTH = 3
N_MEM = 256
XA_HEADS = 4
XA_HEAD_DIM = MIX_WIDTH // XA_HEADS
N_EXPERTS = 16
N_GROUPS = 4
GROUP_SIZE = N_EXPERTS // N_GROUPS
D_EXPERT = D_MODEL // 2
ALPHA = (2 * DEPTH) ** 0.25
LN_EPS = 1e-5

N_PAIRS = GROUP_SIZE * (GROUP_SIZE - 1) // 2
N_CLASSES = N_GROUPS * N_PAIRS
CLASS_ROWS = 32
PAIR_LO = (0, 0, 0, 1, 1, 2)
PAIR_HI = (1, 2, 3, 2, 3, 3)

LANES = 128
TM = 512
MOE_TILE = 256
ROW_W = D_MODEL + LANES
META_IDX, META_GA, META_GB = 0, 1, 2
HALO = 8
MIXER_STAGES = 4
N_STAGES = 3
MIXER_SLAB = 256
MIXER_ISSUE_ROWS = 32
MOE_SLAB = 256
VMEM_LIMIT = 56 * 1024 * 1024
KIND_FULL, KIND_PARTIAL, KIND_IDLE = 0, 1, 2

F32 = jnp.float32
BF16 = jnp.bfloat16
I32 = jnp.int32


def _cfg():
    n_tokens = BATCH * SEQ
    n_full_max = n_tokens // MOE_TILE
    n_pool_tiles = n_full_max + N_CLASSES
    assert n_pool_tiles <= TM and SEQ % TM == 0 and TM % SGU_CHUNK == 0
    assert n_tokens // TM >= MIXER_STAGES and n_tokens - N_CLASSES * (MOE_TILE - 1) >= N_STAGES * MOE_TILE
    return types.SimpleNamespace(
        n_tokens=n_tokens, n_mixer_steps=n_tokens // TM, tiles_per_seq=SEQ // TM,
        n_full_max=n_full_max, n_pool_tiles=n_pool_tiles, n_pool_rows=n_pool_tiles * MOE_TILE,
        n_moe_steps=n_full_max + 1 + N_CLASSES)


def _dot(a, b):
    return jnp.dot(a, b, preferred_element_type=F32)


def _dot_nt(a, b):
    return lax.dot_general(a, b, (((1,), (1,)), ((), ())), preferred_element_type=F32)


def _layer_norm(x, g, b):
    mu = jnp.mean(x, axis=-1, keepdims=True)
    xc = x - mu
    var = jnp.mean(xc * xc, axis=-1, keepdims=True)
    return xc * lax.rsqrt(var + LN_EPS) * g + b


def _sigmoid(x):
    return 1.0 / (1.0 + jnp.exp(-x))


def _row_copy(src_ref, src_row, dst_ref, dst_row, sem):
    return pltpu.make_async_copy(src_ref.at[pl.ds(src_row, 1)], dst_ref.at[pl.ds(dst_row, 1)], sem)


def _kv_kernel(mem_ref, wkv_ref, kt_ref, v_ref):
    kv = _dot(mem_ref[0].astype(BF16), wkv_ref[0])
    kt_ref[0, 0] = kv[:, :MIX_WIDTH].T.astype(BF16)
    v_ref[0, 0] = kv[:, MIX_WIDTH:].astype(BF16)


def _kv_call(mem, wkv_bf16):
    return pl.pallas_call(
        _kv_kernel,
        grid=(DEPTH, BATCH),
        in_specs=[
            pl.BlockSpec((1, N_MEM, D_MODEL), lambda l, b: (b, 0, 0)),
            pl.BlockSpec((1, D_MODEL, 2 * MIX_WIDTH), lambda l, b: (l, 0, 0)),
        ],
        out_specs=[
            pl.BlockSpec((1, 1, MIX_WIDTH, N_MEM), lambda l, b: (l, b, 0, 0)),
            pl.BlockSpec((1, 1, N_MEM, MIX_WIDTH), lambda l, b: (l, b, 0, 0)),
        ],
        out_shape=[
            jax.ShapeDtypeStruct((DEPTH, BATCH, MIX_WIDTH, N_MEM), BF16),
            jax.ShapeDtypeStruct((DEPTH, BATCH, N_MEM, MIX_WIDTH), BF16),
        ],
        name="kv_proj",
    )(mem, wkv_bf16)


def _route(y, step, rw_ref, rb_ref, tri_ref, low_ref, cnt_ref, cur_ref, nxt_ref, tbl_ref):
    y_hi = y.astype(BF16)
    y_lo = (y - y_hi.astype(F32)).astype(BF16)
    l_hi = _dot_nt(rw_ref[...], y_hi)
    l_lo = _dot_nt(rw_ref[0:32, :], y_lo)
    logits = l_hi[0:32] + l_hi[32:64] + l_lo
    scores = _sigmoid(logits)
    sel = scores + rb_ref[...]

    row = lax.broadcasted_iota(I32, (8, TM), 0).astype(F32)
    s = [sel[8 * m:8 * m + 8] for m in range(GROUP_SIZE)]
    sc = [scores[8 * m:8 * m + 8] for m in range(GROUP_SIZE)]
    neg = jnp.float32(-jnp.inf)
    m1 = jnp.maximum(jnp.maximum(s[0], s[1]), jnp.maximum(s[2], s[3]))
    i1 = jnp.where(s[0] == m1, 0.0, jnp.where(s[1] == m1, 1.0, jnp.where(s[2] == m1, 2.0, 3.0)))
    t = [jnp.where(i1 == float(m), neg, s[m]) for m in range(GROUP_SIZE)]
    m2 = jnp.maximum(jnp.maximum(t[0], t[1]), jnp.maximum(t[2], t[3]))
    i2 = jnp.where(t[0] == m2, 0.0, jnp.where(t[1] == m2, 1.0, jnp.where(t[2] == m2, 2.0, 3.0)))
    gscore = jnp.where(row < float(N_GROUPS), m1 + m2, neg)
    gmax = jnp.max(gscore, axis=0, keepdims=True)
    top_group = jnp.min(jnp.where(gscore == gmax, row, 8.0), axis=0, keepdims=True)
    lo = jnp.minimum(i1, i2)
    hi = jnp.maximum(i1, i2)
    pair = jnp.where(lo == 0.0, 0.0, jnp.where(lo == 1.0, 3.0, 5.0)) + hi - lo - 1.0
    s_lo = sum(jnp.where(lo == float(m), sc[m], 0.0) for m in range(GROUP_SIZE))
    s_hi = sum(jnp.where(hi == float(m), sc[m], 0.0) for m in range(GROUP_SIZE))
    inv = 1.0 / (s_lo + s_hi)
    chosen = row == top_group
    cls = jnp.sum(jnp.where(chosen, row * float(N_PAIRS) + pair, 0.0), axis=0, keepdims=True)
    ga = jnp.sum(jnp.where(chosen, s_lo * inv, 0.0), axis=0, keepdims=True)
    gb = jnp.sum(jnp.where(chosen, s_hi * inv, 0.0), axis=0, keepdims=True)

    crow = lax.broadcasted_iota(I32, (CLASS_ROWS, TM), 0).astype(F32)
    onehot = jnp.where(crow == cls, 1.0, 0.0)
    before = _dot(onehot.astype(BF16), tri_ref[...])
    cnt = cnt_ref[...]
    cur = cur_ref[...]
    nxt = nxt_ref[0:1, :]
    tile_f = float(MOE_TILE)
    inv_tile = 1.0 / MOE_TILE
    rank = cnt + before
    cnt_new = cnt + jnp.sum(onehot, axis=1, keepdims=True)
    tiles_old = jnp.floor((cnt + (tile_f - 1.0)) * inv_tile)
    tiles_new = jnp.floor((cnt_new + (tile_f - 1.0)) * inv_tile)
    fresh = tiles_new - tiles_old
    base = nxt + _dot(low_ref[...], fresh.astype(BF16))
    q = jnp.floor(rank * inv_tile)
    tile_of = jnp.where(q < tiles_old, cur, base + (q - tiles_old))
    pos_all = tile_of * tile_f + (rank - q * tile_f)
    pos = jnp.sum(onehot * pos_all, axis=0, keepdims=True)

    cnt_ref[...] = cnt_new
    cur_ref[...] = jnp.where(fresh > 0.0, base + fresh - 1.0, cur)
    nxt_ref[...] = jnp.broadcast_to(nxt + jnp.sum(fresh, axis=0, keepdims=True), nxt_ref.shape)
    ids = lax.broadcasted_iota(I32, (CLASS_ROWS, TM), 1).astype(F32)
    hit = jnp.where((ids >= base) & (ids < base + fresh), 1.0, 0.0)
    cls_hit = jnp.sum(hit * crow, axis=0, keepdims=True)
    any_hit = jnp.sum(hit, axis=0, keepdims=True)
    tbl_ref[...] = jnp.where(any_hit > 0.0, cls_hit, tbl_ref[...])

    idx = (lax.broadcasted_iota(I32, (1, TM), 1) + step * TM).astype(F32)
    return pos, idx, ga, gb


def _make_mixer_kernel(mixer, cfg):
    def body(*refs):
        if mixer == "gmlp":
            (x_ref, win_ref, lng_ref, lnb_ref, sw_ref, sb_ref, *rest) = refs
        else:
            (x_ref, xh_ref, win_ref, cw_ref, *rest) = refs
        (kt_ref, v_ref, wout_ref, g1_ref, b1_ref, rw_ref, rb_ref,
         xs_ref, cnt_out, cur_out, nxt_out, tbl_out,
         tri_ref, low_ref, cnt_ref, cur_ref, nxt_ref, tbl_ref, buf_ref, posv_ref, poss_ref,
         stv_ref, sts_ref, scat_sem, pos_sem, st_sem) = rest

        step = pl.program_id(0)
        last = pl.num_programs(0) - 1
        slot = step % MIXER_STAGES
        out_stage = (step + MIXER_STAGES - 2) % MIXER_STAGES
        pbuf = step % 2
        zbuf = buf_ref.at[pl.ds(MIXER_STAGES * TM, TM)]
        catbuf = buf_ref.at[pl.ds((MIXER_STAGES + 1) * TM, TM), pl.ds(0, D_MODEL)]
        obuf = buf_ref.at[pl.ds((MIXER_STAGES + 1) * TM, TM), pl.ds(D_MODEL, D_MODEL)]

        def stage_copy(stage, r, b):
            tile = buf_ref.at[pl.ds(pl.multiple_of(stage * TM, TM), TM), pl.ds(0, ROW_W)]
            return pltpu.make_async_copy(tile.at[pl.ds(r, 1)], xs_ref.at[pl.ds(poss_ref[b, r], 1)],
                                         scat_sem.at[stage])

        def pos_copy(b):
            return pltpu.make_async_copy(posv_ref.at[pl.ds(b, 1)], poss_ref.at[pl.ds(b, 1)], pos_sem.at[b])

        def stage_wait(stage):
            pltpu.make_async_copy(buf_ref.at[pl.ds(0, TM), pl.ds(0, ROW_W)], xs_ref.at[pl.ds(0, TM)],
                                  scat_sem.at[stage]).wait()

        @pl.when(step == 0)
        def _():
            r = lax.broadcasted_iota(I32, (TM, TM), 0)
            c = lax.broadcasted_iota(I32, (TM, TM), 1)
            tri_ref[...] = jnp.where(r < c, 1.0, 0.0).astype(BF16)
            r = lax.broadcasted_iota(I32, (CLASS_ROWS, CLASS_ROWS), 0)
            c = lax.broadcasted_iota(I32, (CLASS_ROWS, CLASS_ROWS), 1)
            low_ref[...] = jnp.where(c < r, 1.0, 0.0).astype(BF16)
            cnt_ref[...] = jnp.zeros_like(cnt_ref)
            cur_ref[...] = jnp.full_like(cur_ref, -1.0)
            nxt_ref[...] = jnp.zeros_like(nxt_ref)
            tbl_ref[...] = jnp.full_like(tbl_ref, -1.0)
            buf_ref[pl.ds((MIXER_STAGES - 2) * TM, 2 * TM), 0:ROW_W] = jnp.zeros((2 * TM, ROW_W), F32)

            def init_pos(r, carry):
                poss_ref[0, r] = r
                poss_ref[1, r] = r
                return carry

            lax.fori_loop(0, TM, init_pos, 0)

        @pl.when(step >= 2)
        def _():
            pos_copy(pbuf).wait()

        issued = [0]

        def issue_next():
            lo = issued[0]
            hi = min(lo + MIXER_ISSUE_ROWS, TM)
            for r in range(lo, hi):
                stage_copy(out_stage, r, pbuf).start()
            issued[0] = hi

        x = x_ref[...]
        xb = x.astype(BF16)
        for n in range(win_ref.shape[-1] // MIXER_SLAB):
            cols = slice(n * MIXER_SLAB, (n + 1) * MIXER_SLAB)
            issue_next()
            zbuf[:, cols] = _dot(xb, win_ref[0, :, cols])
        if mixer == "gmlp":
            zuv = zbuf[:, 0:2 * MIX_WIDTH]
            uv = zuv * (lax.erf(zuv * (2.0 ** -0.5)) + 1.0) * 0.5
            u = uv[:, :MIX_WIDTH]
            vn = _layer_norm(uv[:, MIX_WIDTH:], lng_ref[0], lnb_ref[0]).astype(BF16)
            pos_r = lax.broadcasted_iota(I32, (SGU_CHUNK, SGU_CHUNK), 0) // CHUNK
            pos_c = lax.broadcasted_iota(I32, (SGU_CHUNK, SGU_CHUNK), 1) // CHUNK
            causal = pos_r >= pos_c
            for c in range(TM // SGU_CHUNK):
                rows = slice(c * SGU_CHUNK, (c + 1) * SGU_CHUNK)
                blocks = []
                for h in range(SGU_HEADS):
                    w = jnp.where(causal, sw_ref[0, h], 0.0).astype(BF16)
                    blk = vn[rows, h * SGU_HEAD_DIM:(h + 1) * SGU_HEAD_DIM]
                    blocks.append(_dot(w, blk) + sb_ref[0, h])
                issue_next()
                catbuf[rows, 0:MIX_WIDTH] = u[rows] * jnp.concatenate(blocks, axis=1)
            q = zbuf[:, 2 * MIX_WIDTH:3 * MIX_WIDTH]
        else:
            bg = zbuf[:, 0:MIX_WIDTH]
            hcur = zbuf[:, MIX_WIDTH:2 * MIX_WIDTH] * zbuf[:, 2 * MIX_WIDTH:3 * MIX_WIDTH]
            zh = _dot(xh_ref[...].astype(BF16), win_ref[0, :, MIX_WIDTH:3 * MIX_WIDTH])
            first_of_seq = (step % cfg.tiles_per_seq) == 0
            hprev = jnp.where(first_of_seq, 0.0, zh[:, :MIX_WIDTH] * zh[:, MIX_WIDTH:])
            rowid = lax.broadcasted_iota(I32, (TM, MIX_WIDTH), 0)
            h1 = jnp.where(rowid == 0, hprev[HALO - 1:HALO], pltpu.roll(hcur, 1, 0))
            h2 = jnp.where(rowid == 0, hprev[HALO - 2:HALO - 1],
                           jnp.where(rowid == 1, hprev[HALO - 1:HALO], pltpu.roll(hcur, 2, 0)))
            cw = cw_ref[0]
            issue_next()
            catbuf[:, 0:MIX_WIDTH] = bg * (cw[0:1] * h2 + cw[1:2] * h1 + cw[2:3] * hcur)
            q = zbuf[:, 3 * MIX_WIDTH:4 * MIX_WIDTH]

        scale = XA_HEAD_DIM ** -0.5
        for h in range(XA_HEADS):
            lo, hi = h * XA_HEAD_DIM, (h + 1) * XA_HEAD_DIM
            sc = _dot(q[:, lo:hi].astype(BF16), kt_ref[0, 0, lo:hi, :]) * scale
            e = jnp.exp(sc - jnp.max(sc, axis=-1, keepdims=True))
            p = e * (1.0 / jnp.sum(e, axis=-1, keepdims=True))
            issue_next()
            catbuf[:, MIX_WIDTH + lo:MIX_WIDTH + hi] = _dot(p.astype(BF16), v_ref[0, 0, :, lo:hi])
        cat = catbuf[:, 0:D_MODEL].astype(BF16)
        for n in range(D_MODEL // MIXER_SLAB):
            cols = slice(n * MIXER_SLAB, (n + 1) * MIXER_SLAB)
            issue_next()
            obuf[:, cols] = _dot(cat, wout_ref[0, :, cols])
        while issued[0] < TM:
            issue_next()
        y = _layer_norm(ALPHA * x + obuf[...], g1_ref[0], b1_ref[0])

        pos, idx, ga, gb = _route(y, step, rw_ref, rb_ref, tri_ref, low_ref, cnt_ref, cur_ref, nxt_ref, tbl_ref)
        mrow = lax.broadcasted_iota(I32, (LANES, TM), 0)
        meta = jnp.where(mrow == META_IDX, idx,
                         jnp.where(mrow == META_GA, ga, jnp.where(mrow == META_GB, gb, 0.0)))
        meta_t = jnp.transpose(meta)
        posv_ref[pl.ds(pbuf, 1), :] = pos.astype(I32)
        cnt_out[...] = cnt_ref[:, 0:LANES]
        cur_out[...] = cur_ref[:, 0:LANES]
        nxt_out[...] = nxt_ref[:, 0:LANES]
        tbl_out[...] = tbl_ref[...]

        @pl.when(step < 2)
        def _():
            stage_wait(out_stage)

        @pl.when(step >= MIXER_STAGES)
        def _():
            stage_wait(slot)

        own = pl.ds(pl.multiple_of(slot * TM, TM), TM)
        buf_ref[own, 0:D_MODEL] = y
        buf_ref[own, D_MODEL:ROW_W] = meta_t
        pos_copy(pbuf).start()

        @pl.when(step == last)
        def _():
            pos_copy(0).wait()
            pos_copy(1).wait()
            prev_stage = (step + MIXER_STAGES - 1) % MIXER_STAGES

            def issue_last(r, carry):
                stage_copy(prev_stage, r, 1 - pbuf).start()
                stage_copy(slot, r, pbuf).start()
                return carry

            lax.fori_loop(0, TM, issue_last, 0, unroll=8)
            lane = lax.broadcasted_iota(I32, (CLASS_ROWS, LANES), 1).astype(F32)
            crow = lax.broadcasted_iota(I32, (CLASS_ROWS, LANES), 0).astype(F32)
            diag = crow == lane
            cnt_l = jnp.sum(jnp.where(diag, cnt_ref[:, 0:LANES], 0.0), axis=0, keepdims=True)
            cur_l = jnp.sum(jnp.where(diag, cur_ref[:, 0:LANES], 0.0), axis=0, keepdims=True)
            srow = lax.broadcasted_iota(I32, (8, LANES), 0)
            st = jnp.where(srow == 0, cnt_l, jnp.where(srow == 1, cur_l, nxt_ref[:, 0:LANES]))
            stv_ref[...] = st.astype(I32)
            cp = pltpu.make_async_copy(stv_ref, sts_ref, st_sem)
            cp.start()
            cp.wait()
            for stage in range(MIXER_STAGES):
                stage_wait(stage)
            zero_row = pl.multiple_of(out_stage * TM, TM)
            buf_ref[pl.ds(zero_row, TM), 0:ROW_W] = jnp.zeros((TM, ROW_W), F32)
            zrow = buf_ref.at[pl.ds(zero_row, MOE_TILE), pl.ds(0, ROW_W)]
            zsem = scat_sem.at[0]

            def clear_class(c, carry):
                fill = sts_ref[0, c] % MOE_TILE
                start = sts_ref[1, c] * MOE_TILE + fill
                n = jnp.where(fill == 0, 0, MOE_TILE - fill)

                def one(j, carry2):
                    _row_copy(zrow, 0, xs_ref, start + j, zsem).start()
                    return carry2

                def one_wait(j, carry2):
                    _row_copy(zrow, 0, xs_ref, 0, zsem).wait()
                    return carry2

                lax.fori_loop(0, n, one, 0)
                lax.fori_loop(0, n, one_wait, 0)
                return carry

            lax.fori_loop(0, N_CLASSES, clear_class, 0)

            def clear_tile(t, carry):
                start = pl.multiple_of(t * MOE_TILE, MOE_TILE)
                cp2 = pltpu.make_async_copy(zrow, xs_ref.at[pl.ds(start, MOE_TILE)], zsem)
                cp2.start()
                cp2.wait()
                return carry

            lax.fori_loop(sts_ref[2, 0], cfg.n_pool_tiles, clear_tile, 0)

    return body


def _mixer_call(mixer, x, p, layer, j):
    cfg = _cfg()

    def const_spec(shape):
        return pl.BlockSpec(shape, lambda i: (0,) * len(shape))

    def layer_spec(shape, l):
        return pl.BlockSpec((1,) + shape, lambda i: (l,) + (0,) * len(shape))

    x_spec = pl.BlockSpec((TM, D_MODEL), lambda i: (i, 0))
    if mixer == "gmlp":
        head_specs = [
            x_spec,
            layer_spec((D_MODEL, 3 * MIX_WIDTH), j),
            layer_spec((1, MIX_WIDTH), j),
            layer_spec((1, MIX_WIDTH), j),
            layer_spec((SGU_HEADS, SGU_CHUNK, SGU_CHUNK), j),
            layer_spec((SGU_HEADS, SGU_CHUNK, SGU_HEAD_DIM), j),
        ]
        head_args = (x, p["w_in_a"], p["sgu_ln_g"], p["sgu_ln_b"], p["sgu_w"], p["sgu_b_col"])
    else:
        head_specs = [
            x_spec,
            pl.BlockSpec((HALO, D_MODEL), lambda i: (jnp.maximum(i * (TM // HALO) - 1, 0), 0)),
            layer_spec((D_MODEL, 4 * MIX_WIDTH), j),
            layer_spec((CONV_WIDTH, MIX_WIDTH), j),
        ]
        head_args = (x, x, p["w_in_b"], p["conv_w"])
    tail_specs = [
        pl.BlockSpec((1, 1, MIX_WIDTH, N_MEM), lambda i: (layer, i // cfg.tiles_per_seq, 0, 0)),
        pl.BlockSpec((1, 1, N_MEM, MIX_WIDTH), lambda i: (layer, i // cfg.tiles_per_seq, 0, 0)),
        layer_spec((D_MODEL, D_MODEL), layer),
        layer_spec((1, D_MODEL), layer),
        layer_spec((1, D_MODEL), layer),
        const_spec((64, D_MODEL)),
        const_spec((32, 1)),
    ]
    tail_args = (p["kt"], p["v"], p["w_out"], p["ln1_g"], p["ln1_b"], p["rw"], p["rb"])
    return pl.pallas_call(
        _make_mixer_kernel(mixer, cfg),
        grid=(cfg.n_mixer_steps,),
        in_specs=head_specs + tail_specs,
        out_specs=[
            pl.BlockSpec(memory_space=pl.ANY),
            const_spec((CLASS_ROWS, LANES)),
            const_spec((CLASS_ROWS, LANES)),
            const_spec((8, LANES)),
            const_spec((8, TM)),
        ],
        out_shape=[
            jax.ShapeDtypeStruct((cfg.n_pool_rows, ROW_W), F32),
            jax.ShapeDtypeStruct((CLASS_ROWS, LANES), F32),
            jax.ShapeDtypeStruct((CLASS_ROWS, LANES), F32),
            jax.ShapeDtypeStruct((8, LANES), F32),
            jax.ShapeDtypeStruct((8, TM), F32),
        ],
        scratch_shapes=[
            pltpu.VMEM((TM, TM), BF16),
            pltpu.VMEM((CLASS_ROWS, CLASS_ROWS), BF16),
            pltpu.VMEM((CLASS_ROWS, TM), F32),
            pltpu.VMEM((CLASS_ROWS, TM), F32),
            pltpu.VMEM((8, TM), F32),
            pltpu.VMEM((8, TM), F32),
            pltpu.VMEM(((MIXER_STAGES + 2) * TM, 2 * D_MODEL), F32),
            pltpu.VMEM((2, TM), I32),
            pltpu.SMEM((2, TM), I32),
            pltpu.VMEM((8, LANES), I32),
            pltpu.SMEM((8, LANES), I32),
            pltpu.SemaphoreType.DMA((MIXER_STAGES,)),
            pltpu.SemaphoreType.DMA((2,)),
            pltpu.SemaphoreType.DMA(()),
        ],
        compiler_params=pltpu.CompilerParams(dimension_semantics=("arbitrary",), vmem_limit_bytes=VMEM_LIMIT),
        name="mixer_" + mixer,
    )(*head_args, *tail_args)


def _make_moe_kernel(cfg):
    n_slabs = D_MODEL // MOE_SLAB
    rows_per_chunk = MOE_TILE // (4 * n_slabs)
    work_row, facc_row = N_STAGES * MOE_TILE, (N_STAGES + 1) * MOE_TILE

    def body(blk_ref, ea_ref, eb_ref, kind_ref, nvalid_ref, nfull_ref,
             xs_ref, wgu_a_ref, wd_a_ref, wgu_b_ref, wd_b_ref, g2_ref, b2_ref, out_ref,
             buf_ref, idxv_ref, idxs_ref, scat_sem, idx_sem):
        step = pl.program_id(0)
        slot = step % N_STAGES
        prev = (step + N_STAGES - 1) % N_STAGES
        work = buf_ref.at[pl.ds(work_row, MOE_TILE)]
        facc = buf_ref.at[pl.ds(facc_row, MOE_TILE)]

        ibuf = step % 2
        pbuf = 1 - ibuf

        def stage_copy(stage, r, b):
            tile = buf_ref.at[pl.ds(pl.multiple_of(stage * MOE_TILE, MOE_TILE), MOE_TILE)]
            return _row_copy(tile, r, out_ref, idxs_ref[b * 8, r], scat_sem.at[stage])

        def stage_wait(stage):
            pltpu.make_async_copy(buf_ref.at[pl.ds(0, MOE_TILE)], out_ref.at[pl.ds(0, MOE_TILE)],
                                  scat_sem.at[stage]).wait()

        def experts(issue):
            x = xs_ref[:, 0:D_MODEL]
            xb = x.astype(BF16)
            meta = xs_ref[:, D_MODEL:ROW_W]
            gates = (meta[:, META_GA:META_GA + 1], meta[:, META_GB:META_GB + 1])
            chunk = 0
            for k, (wgu_ref, wd_ref) in enumerate(((wgu_a_ref, wd_a_ref), (wgu_b_ref, wd_b_ref))):
                for n in range(n_slabs):
                    cols = slice(n * MOE_SLAB, (n + 1) * MOE_SLAB)
                    issue(chunk)
                    chunk += 1
                    work[:, cols] = _dot(xb, wgu_ref[0, 0, :, cols])
                g = work[:, :D_EXPERT]
                hb = (g * _sigmoid(g) * work[:, D_EXPERT:]).astype(BF16)
                for n in range(n_slabs):
                    cols = slice(n * MOE_SLAB, (n + 1) * MOE_SLAB)
                    issue(chunk)
                    chunk += 1
                    part = gates[k] * _dot(hb, wd_ref[0, 0, :, cols])
                    facc[:, cols] = part if k == 0 else facc[:, cols] + part
            return _layer_norm(ALPHA * x + facc[...], g2_ref[0], b2_ref[0])

        def idx_copy(b):
            dst = idxs_ref.at[pl.ds(pl.multiple_of(b * 8, 8), 8)]
            return pltpu.make_async_copy(idxv_ref, dst, idx_sem.at[b])

        def put_idx(b):
            idx = jnp.transpose(xs_ref[:, D_MODEL:ROW_W])[META_IDX:META_IDX + 1, :]
            idxv_ref[...] = jnp.broadcast_to(idx, idxv_ref.shape).astype(I32)
            return idx_copy(b)

        @pl.when(step == 0)
        def _():
            buf_ref[pl.ds((N_STAGES - 1) * MOE_TILE, MOE_TILE), :] = jnp.zeros((MOE_TILE, D_MODEL), F32)

            def init_idx(r, carry):
                idxs_ref[8, r] = r
                return carry

            lax.fori_loop(0, MOE_TILE, init_idx, 0)

        @pl.when(kind_ref[step] == KIND_FULL)
        def _():
            @pl.when(step > 0)
            def _():
                idx_copy(pbuf).wait()

            @pl.when(step < nfull_ref[0])
            def _():
                put_idx(ibuf).start()

            def issue(chunk):
                for r in range(chunk * rows_per_chunk, (chunk + 1) * rows_per_chunk):
                    stage_copy(prev, r, pbuf).start()

            y = experts(issue)

            @pl.when(step == 0)
            def _():
                stage_wait(prev)

            @pl.when(step >= N_STAGES)
            def _():
                stage_wait(slot)

            buf_ref[pl.ds(pl.multiple_of(slot * MOE_TILE, MOE_TILE), MOE_TILE), :] = y

            @pl.when(step == nfull_ref[0])
            def _():
                stage_wait((step + 1) % N_STAGES)
                stage_wait(prev)

        @pl.when(kind_ref[step] == KIND_PARTIAL)
        def _():
            cp = put_idx(0)
            cp.start()
            buf_ref[pl.ds(0, MOE_TILE), :] = experts(lambda chunk: None)
            cp.wait()
            n = nvalid_ref[step]

            def one(r, carry):
                stage_copy(0, r, 0).start()
                return carry

            def one_wait(r, carry):
                _row_copy(buf_ref, 0, out_ref, 0, scat_sem.at[0]).wait()
                return carry

            lax.fori_loop(0, n, one, 0)
            lax.fori_loop(0, n, one_wait, 0)

    return body


def _moe_call(plan, xs, p, layer):
    cfg = _cfg()

    def xmap(i, blk, *_):
        return (blk[i], 0)

    def wa(i, blk, ea, *_):
        return (layer, ea[i], 0, 0)

    def wb(i, blk, ea, eb, *_):
        return (layer, eb[i], 0, 0)

    def lmap(i, *_):
        return (layer, 0, 0)

    return pl.pallas_call(
        _make_moe_kernel(cfg),
        grid_spec=pltpu.PrefetchScalarGridSpec(
            num_scalar_prefetch=6,
            grid=(cfg.n_moe_steps,),
            in_specs=[
                pl.BlockSpec((MOE_TILE, ROW_W), xmap),
                pl.BlockSpec((1, 1, D_MODEL, 2 * D_EXPERT), wa),
                pl.BlockSpec((1, 1, D_EXPERT, D_MODEL), wa),
                pl.BlockSpec((1, 1, D_MODEL, 2 * D_EXPERT), wb),
                pl.BlockSpec((1, 1, D_EXPERT, D_MODEL), wb),
                pl.BlockSpec((1, 1, D_MODEL), lmap),
                pl.BlockSpec((1, 1, D_MODEL), lmap),
            ],
            out_specs=pl.BlockSpec(memory_space=pl.ANY),
            scratch_shapes=[
                pltpu.VMEM(((N_STAGES + 2) * MOE_TILE, D_MODEL), F32),
                pltpu.VMEM((8, MOE_TILE), I32),
                pltpu.SMEM((16, MOE_TILE), I32),
                pltpu.SemaphoreType.DMA((N_STAGES,)),
                pltpu.SemaphoreType.DMA((2,)),
            ],
        ),
        out_shape=jax.ShapeDtypeStruct((cfg.n_tokens, D_MODEL), F32),
        compiler_params=pltpu.CompilerParams(dimension_semantics=("arbitrary",), vmem_limit_bytes=VMEM_LIMIT),
        name="moe_experts",
    )(*plan, xs, p["w_gu"], p["w_down"], p["w_gu"], p["w_down"], p["ln2_g"], p["ln2_b"])


def _moe_plan(cnt_out, cur_out, nxt_out, tbl_out):
    cfg = _cfg()
    cnt = cnt_out[:N_CLASSES, 0].astype(I32)
    cur = cur_out[:N_CLASSES, 0].astype(I32)
    n_used = nxt_out[0, 0].astype(I32)
    tile_cls = jnp.clip(tbl_out[0, :cfg.n_pool_tiles].astype(I32), 0, N_CLASSES - 1)
    tid = jnp.arange(cfg.n_pool_tiles, dtype=I32)
    fill = cnt % MOE_TILE
    ends_here = (cur[None, :] == tid[:, None]) & (fill[None, :] != 0)
    is_partial = jnp.any(ends_here, axis=1)
    n_valid = jnp.where(is_partial, jnp.sum(jnp.where(ends_here, fill[None, :], 0), axis=1), MOE_TILE)
    used = tid < n_used
    big = jnp.int32(1 << 30)
    key = tile_cls * (2 * cfg.n_pool_tiles) + tid
    full = used & ~is_partial
    part = used & is_partial
    def order_of(keys):
        rank = jnp.sum(keys[None, :] < keys[:, None], axis=1)
        return jnp.sum(jnp.where(rank[None, :] == tid[:, None], tid[None, :], 0), axis=1).astype(I32)

    order_full = order_of(jnp.where(full, key, big + tid))
    order_part = order_of(jnp.where(part, key, big + tid))
    n_full = jnp.sum(full).astype(I32)
    n_part = jnp.sum(part).astype(I32)
    last_full = order_full[n_full - 1]

    i = jnp.arange(cfg.n_full_max + 1, dtype=I32)
    blk_a = order_full[jnp.minimum(i, n_full - 1)]
    kind_a = jnp.where(i <= n_full, KIND_FULL, KIND_IDLE)
    k = jnp.arange(N_CLASSES, dtype=I32)
    blk_b = jnp.where(n_part > 0, order_part[jnp.clip(jnp.minimum(k, n_part - 1), 0, None)], last_full)
    kind_b = jnp.where(k < n_part, KIND_PARTIAL, KIND_IDLE)
    blk = jnp.concatenate([blk_a, blk_b]).astype(I32)
    kind = jnp.concatenate([kind_a, kind_b]).astype(I32)
    cls = tile_cls[blk]
    base = (cls // N_PAIRS) * GROUP_SIZE
    ea = base + jnp.asarray(PAIR_LO, I32)[cls % N_PAIRS]
    eb = base + jnp.asarray(PAIR_HI, I32)[cls % N_PAIRS]
    return blk, ea.astype(I32), eb.astype(I32), kind, n_valid[blk].astype(I32), n_full.reshape(1)


def _router_rows(router_w):
    wt = router_w.astype(F32).T.reshape(N_GROUPS, GROUP_SIZE, D_MODEL)
    wt = jnp.transpose(wt, (1, 0, 2))
    wt = jnp.pad(wt, ((0, 0), (0, 8 - N_GROUPS), (0, 0))).reshape(32, D_MODEL)
    hi = wt.astype(BF16)
    lo = (wt - hi.astype(F32)).astype(BF16)
    return jnp.concatenate([hi, lo], axis=0)


def _router_bias_rows(router_b):
    b = jnp.transpose(router_b.astype(F32).reshape(N_GROUPS, GROUP_SIZE), (1, 0))
    return jnp.pad(b, ((0, 0), (0, 8 - N_GROUPS))).reshape(32, 1)


def kernel(x, mem, w_in_a, sgu_ln_g, sgu_ln_b, sgu_w, sgu_b, w_in_b, conv_w, w_kv, w_out, ln1_g, ln1_b,
           router_w, router_b, w_gate, w_up, w_down, ln2_g, ln2_b):
    cfg = _cfg()
    kt, v = _kv_call(mem, w_kv.astype(BF16))
    p = {
        "w_in_a": w_in_a.astype(BF16),
        "sgu_ln_g": sgu_ln_g[:, None, :],
        "sgu_ln_b": sgu_ln_b[:, None, :],
        "sgu_w": sgu_w,
        "sgu_b_col": jnp.broadcast_to(sgu_b[..., None], sgu_b.shape + (SGU_HEAD_DIM,)),
        "w_in_b": w_in_b.astype(BF16),
        "conv_w": conv_w,
        "kt": kt,
        "v": v,
        "w_out": w_out.astype(BF16),
        "ln1_g": ln1_g[:, None, :],
        "ln1_b": ln1_b[:, None, :],
        "rw": _router_rows(router_w),
        "rb": _router_bias_rows(router_b),
        "w_gu": jnp.concatenate([w_gate, w_up], axis=-1).astype(BF16),
        "w_down": w_down.astype(BF16),
        "ln2_g": ln2_g[:, None, :],
        "ln2_b": ln2_b[:, None, :],
    }
    h = x.reshape(cfg.n_tokens, D_MODEL)
    for layer in range(DEPTH):
        mixer = "gmlp" if layer % 2 == 0 else "conv"
        xs, cnt, cur, nxt, tbl = _mixer_call(mixer, h, p, layer, layer // 2)
        h = _moe_call(_moe_plan(cnt, cur, nxt, tbl), xs, p, layer)
    return h.reshape(BATCH, SEQ, D_MODEL)
```

```python
import types

import jax
import jax.numpy as jnp
from jax import lax
from jax.experimental import pallas as pl
from jax.experimental.pallas import tpu as pltpu

D_MODEL = 1024
BATCH = 16
SEQ = 4096
DEPTH = 2
CHUNK = 64
MIX_WIDTH = D_MODEL // 2
SGU_CHUNK = 128
SGU_HEADS = 4
SGU_HEAD_DIM = MIX_WIDTH // SGU_HEADS
CONV_WIDTH = 3
N_MEM = 256
XA_HEADS = 4
XA_HEAD_DIM = MIX_WIDTH // XA_HEADS
N_EXPERTS = 16
N_GROUPS = 4
GROUP_SIZE = N_EXPERTS // N_GROUPS
D_EXPERT = D_MODEL // 2
ALPHA = (2 * DEPTH) ** 0.25
LN_EPS = 1e-5

N_PAIRS = GROUP_SIZE * (GROUP_SIZE - 1) // 2
N_CLASSES = N_GROUPS * N_PAIRS
CLASS_ROWS = 32
PAIR_LO = (0, 0, 0, 1, 1, 2)
PAIR_HI = (1, 2, 3, 2, 3, 3)

LANES = 128
TM = 512
MOE_TILE = 256
ROW_W = D_MODEL + LANES
META_IDX, META_GA, META_GB = 0, 1, 2
HALO = 8
MIXER_STAGES = 4
N_STAGES = 3
MIXER_SLAB = 256
MIXER_ISSUE_ROWS = 32
MOE_SLAB = 256
VMEM_LIMIT = 56 * 1024 * 1024
KIND_FULL, KIND_PARTIAL, KIND_IDLE = 0, 1, 2

F32 = jnp.float32
BF16 = jnp.bfloat16
I32 = jnp.int32


def _cfg():
    n_tokens = BATCH * SEQ
    n_full_max = n_tokens // MOE_TILE
    n_pool_tiles = n_full_max + N_CLASSES
    assert n_pool_tiles <= TM and SEQ % TM == 0 and TM % SGU_CHUNK == 0
    assert n_tokens // TM >= MIXER_STAGES and n_tokens - N_CLASSES * (MOE_TILE - 1) >= N_STAGES * MOE_TILE
    return types.SimpleNamespace(
        n_tokens=n_tokens, n_mixer_steps=n_tokens // TM, tiles_per_seq=SEQ // TM,
        n_full_max=n_full_max, n_pool_tiles=n_pool_tiles, n_pool_rows=n_pool_tiles * MOE_TILE,
        n_moe_steps=n_full_max + 1 + N_CLASSES)


def _dot(a, b):
    return jnp.dot(a, b, preferred_element_type=F32)


def _dot_nt(a, b):
    return lax.dot_general(a, b, (((1,), (1,)), ((), ())), preferred_element_type=F32)


def _layer_norm(x, g, b):
    mu = jnp.mean(x, axis=-1, keepdims=True)
    xc = x - mu
    var = jnp.mean(xc * xc, axis=-1, keepdims=True)
    return xc * lax.rsqrt(var + LN_EPS) * g + b


def _sigmoid(x):
    return 1.0 / (1.0 + jnp.exp(-x))


def _row_copy(src_ref, src_row, dst_ref, dst_row, sem):
    return pltpu.make_async_copy(src_ref.at[pl.ds(src_row, 1)], dst_ref.at[pl.ds(dst_row, 1)], sem)


def _kv_kernel(mem_ref, wkv_ref, kt_ref, v_ref):
    kv = _dot(mem_ref[0].astype(BF16), wkv_ref[0])
    kt_ref[0, 0] = kv[:, :MIX_WIDTH].T.astype(BF16)
    v_ref[0, 0] = kv[:, MIX_WIDTH:].astype(BF16)


def _kv_call(mem, wkv_bf16):
    return pl.pallas_call(
        _kv_kernel,
        grid=(DEPTH, BATCH),
        in_specs=[
            pl.BlockSpec((1, N_MEM, D_MODEL), lambda l, b: (b, 0, 0)),
            pl.BlockSpec((1, D_MODEL, 2 * MIX_WIDTH), lambda l, b: (l, 0, 0)),
        ],
        out_specs=[
            pl.BlockSpec((1, 1, MIX_WIDTH, N_MEM), lambda l, b: (l, b, 0, 0)),
            pl.BlockSpec((1, 1, N_MEM, MIX_WIDTH), lambda l, b: (l, b, 0, 0)),
        ],
        out_shape=[
            jax.ShapeDtypeStruct((DEPTH, BATCH, MIX_WIDTH, N_MEM), BF16),
            jax.ShapeDtypeStruct((DEPTH, BATCH, N_MEM, MIX_WIDTH), BF16),
        ],
        name="kv_proj",
    )(mem, wkv_bf16)


def _route(y, step, rw_ref, rb_ref, tri_ref, low_ref, cnt_ref, cur_ref, nxt_ref, tbl_ref):
    y_hi = y.astype(BF16)
    y_lo = (y - y_hi.astype(F32)).astype(BF16)
    l_hi = _dot_nt(rw_ref[...], y_hi)
    l_lo = _dot_nt(rw_ref[0:32, :], y_lo)
    logits = l_hi[0:32] + l_hi[32:64] + l_lo
    scores = _sigmoid(logits)
    sel = scores + rb_ref[...]

    row = lax.broadcasted_iota(I32, (8, TM), 0).astype(F32)
    s = [sel[8 * m:8 * m + 8] for m in range(GROUP_SIZE)]
    sc = [scores[8 * m:8 * m + 8] for m in range(GROUP_SIZE)]
    neg = jnp.float32(-jnp.inf)
    m1 = jnp.maximum(jnp.maximum(s[0], s[1]), jnp.maximum(s[2], s[3]))
    i1 = jnp.where(s[0] == m1, 0.0, jnp.where(s[1] == m1, 1.0, jnp.where(s[2] == m1, 2.0, 3.0)))
    t = [jnp.where(i1 == float(m), neg, s[m]) for m in range(GROUP_SIZE)]
    m2 = jnp.maximum(jnp.maximum(t[0], t[1]), jnp.maximum(t[2], t[3]))
    i2 = jnp.where(t[0] == m2, 0.0, jnp.where(t[1] == m2, 1.0, jnp.where(t[2] == m2, 2.0, 3.0)))
    gscore = jnp.where(row < float(N_GROUPS), m1 + m2, neg)
    gmax = jnp.max(gscore, axis=0, keepdims=True)
    top_group = jnp.min(jnp.where(gscore == gmax, row, 8.0), axis=0, keepdims=True)
    lo = jnp.minimum(i1, i2)
    hi = jnp.maximum(i1, i2)
    pair = jnp.where(lo == 0.0, 0.0, jnp.where(lo == 1.0, 3.0, 5.0)) + hi - lo - 1.0
    s_lo = sum(jnp.where(lo == float(m), sc[m], 0.0) for m in range(GROUP_SIZE))
    s_hi = sum(jnp.where(hi == float(m), sc[m], 0.0) for m in range(GROUP_SIZE))
    inv = 1.0 / (s_lo + s_hi)
    chosen = row == top_group
    cls = jnp.sum(jnp.where(chosen, row * float(N_PAIRS) + pair, 0.0), axis=0, keepdims=True)
    ga = jnp.sum(jnp.where(chosen, s_lo * inv, 0.0), axis=0, keepdims=True)
    gb = jnp.sum(jnp.where(chosen, s_hi * inv, 0.0), axis=0, keepdims=True)

    crow = lax.broadcasted_iota(I32, (CLASS_ROWS, TM), 0).astype(F32)
    onehot = jnp.where(crow == cls, 1.0, 0.0)
    before = _dot(onehot.astype(BF16), tri_ref[...])
    cnt = cnt_ref[...]
    cur = cur_ref[...]
    nxt = nxt_ref[0:1, :]
    tile_f = float(MOE_TILE)
    inv_tile = 1.0 / MOE_TILE
    rank = cnt + before
    cnt_new = cnt + jnp.sum(onehot, axis=1, keepdims=True)
    tiles_old = jnp.floor((cnt + (tile_f - 1.0)) * inv_tile)
    tiles_new = jnp.floor((cnt_new + (tile_f - 1.0)) * inv_tile)
    fresh = tiles_new - tiles_old
    base = nxt + _dot(low_ref[...], fresh.astype(BF16))
    q = jnp.floor(rank * inv_tile)
    tile_of = jnp.where(q < tiles_old, cur, base + (q - tiles_old))
    pos_all = tile_of * tile_f + (rank - q * tile_f)
    pos = jnp.sum(onehot * pos_all, axis=0, keepdims=True)

    cnt_ref[...] = cnt_new
    cur_ref[...] = jnp.where(fresh > 0.0, base + fresh - 1.0, cur)
    nxt_ref[...] = jnp.broadcast_to(nxt + jnp.sum(fresh, axis=0, keepdims=True), nxt_ref.shape)
    ids = lax.broadcasted_iota(I32, (CLASS_ROWS, TM), 1).astype(F32)
    hit = jnp.where((ids >= base) & (ids < base + fresh), 1.0, 0.0)
    cls_hit = jnp.sum(hit * crow, axis=0, keepdims=True)
    any_hit = jnp.sum(hit, axis=0, keepdims=True)
    tbl_ref[...] = jnp.where(any_hit > 0.0, cls_hit, tbl_ref[...])

    idx = (lax.broadcasted_iota(I32, (1, TM), 1) + step * TM).astype(F32)
    return pos, idx, ga, gb


def _make_mixer_kernel(mixer, cfg):
    def body(*refs):
        if mixer == "gmlp":
            (x_ref, win_ref, lng_ref, lnb_ref, sw_ref, sb_ref, *rest) = refs
        else:
            (x_ref, xh_ref, win_ref, cw_ref, *rest) = refs
        (kt_ref, v_ref, wout_ref, g1_ref, b1_ref, rw_ref, rb_ref,
         xs_ref, cnt_out, cur_out, nxt_out, tbl_out,
         tri_ref, low_ref, cnt_ref, cur_ref, nxt_ref, tbl_ref, buf_ref, posv_ref, poss_ref,
         stv_ref, sts_ref, scat_sem, pos_sem, st_sem) = rest

        step = pl.program_id(0)
        last = pl.num_programs(0) - 1
        slot = step % MIXER_STAGES
        out_stage = (step + MIXER_STAGES - 2) % MIXER_STAGES
        pbuf = step % 2
        zbuf = buf_ref.at[pl.ds(MIXER_STAGES * TM, TM)]
        catbuf = buf_ref.at[pl.ds((MIXER_STAGES + 1) * TM, TM), pl.ds(0, D_MODEL)]
        obuf = buf_ref.at[pl.ds((MIXER_STAGES + 1) * TM, TM), pl.ds(D_MODEL, D_MODEL)]

        def stage_copy(stage, r, b):
            tile = buf_ref.at[pl.ds(pl.multiple_of(stage * TM, TM), TM), pl.ds(0, ROW_W)]
            return pltpu.make_async_copy(tile.at[pl.ds(r, 1)], xs_ref.at[pl.ds(poss_ref[b, r], 1)],
                                         scat_sem.at[stage])

        def pos_copy(b):
            return pltpu.make_async_copy(posv_ref.at[pl.ds(b, 1)], poss_ref.at[pl.ds(b, 1)], pos_sem.at[b])

        def stage_wait(stage):
            pltpu.make_async_copy(buf_ref.at[pl.ds(0, TM), pl.ds(0, ROW_W)], xs_ref.at[pl.ds(0, TM)],
                                  scat_sem.at[stage]).wait()

        @pl.when(step == 0)
        def _():
            r = lax.broadcasted_iota(I32, (TM, TM), 0)
            c = lax.broadcasted_iota(I32, (TM, TM), 1)
            tri_ref[...] = jnp.where(r < c, 1.0, 0.0).astype(BF16)
            r = lax.broadcasted_iota(I32, (CLASS_ROWS, CLASS_ROWS), 0)
            c = lax.broadcasted_iota(I32, (CLASS_ROWS, CLASS_ROWS), 1)
            low_ref[...] = jnp.where(c < r, 1.0, 0.0).astype(BF16)
            cnt_ref[...] = jnp.zeros_like(cnt_ref)
            cur_ref[...] = jnp.full_like(cur_ref, -1.0)
            nxt_ref[...] = jnp.zeros_like(nxt_ref)
            tbl_ref[...] = jnp.full_like(tbl_ref, -1.0)
            buf_ref[pl.ds((MIXER_STAGES - 2) * TM, 2 * TM), 0:ROW_W] = jnp.zeros((2 * TM, ROW_W), F32)

            def init_pos(r, carry):
                poss_ref[0, r] = r
                poss_ref[1, r] = r
                return carry

            lax.fori_loop(0, TM, init_pos, 0)

        @pl.when(step >= 2)
        def _():
            pos_copy(pbuf).wait()

        issued = [0]

        def issue_next():
            lo = issued[0]
            hi = min(lo + MIXER_ISSUE_ROWS, TM)
            for r in range(lo, hi):
                stage_copy(out_stage, r, pbuf).start(priority=r % 2)
            issued[0] = hi

        x = x_ref[...]
        xb = x.astype(BF16)
        n_in_slabs = win_ref.shape[-1] // MIXER_SLAB
        q_slabs = MIX_WIDTH // MIXER_SLAB
        for n in list(range(n_in_slabs - q_slabs, n_in_slabs)) + list(range(n_in_slabs - q_slabs)):
            cols = slice(n * MIXER_SLAB, (n + 1) * MIXER_SLAB)
            issue_next()
            zbuf[:, cols] = _dot(xb, win_ref[0, :, cols])

        q = zbuf[:, win_ref.shape[-1] - MIX_WIDTH:win_ref.shape[-1]]
        scale = XA_HEAD_DIM ** -0.5
        for h in range(XA_HEADS):
            lo, hi = h * XA_HEAD_DIM, (h + 1) * XA_HEAD_DIM
            sc = _dot(q[:, lo:hi].astype(BF16), kt_ref[0, 0, lo:hi, :]) * scale
            e = jnp.exp(sc - jnp.max(sc, axis=-1, keepdims=True))
            p = e * (1.0 / jnp.sum(e, axis=-1, keepdims=True))
            issue_next()
            catbuf[:, MIX_WIDTH + lo:MIX_WIDTH + hi] = _dot(p.astype(BF16), v_ref[0, 0, :, lo:hi])

        if mixer == "gmlp":
            zuv = zbuf[:, 0:2 * MIX_WIDTH]
            uv = zuv * (lax.erf(zuv * (2.0 ** -0.5)) + 1.0) * 0.5
            u = uv[:, :MIX_WIDTH]
            vn = _layer_norm(uv[:, MIX_WIDTH:], lng_ref[0], lnb_ref[0]).astype(BF16)
            pos_r = lax.broadcasted_iota(I32, (SGU_CHUNK, SGU_CHUNK), 0) // CHUNK
            pos_c = lax.broadcasted_iota(I32, (SGU_CHUNK, SGU_CHUNK), 1) // CHUNK
            causal = pos_r >= pos_c
            for c in range(TM // SGU_CHUNK):
                rows = slice(c * SGU_CHUNK, (c + 1) * SGU_CHUNK)
                blocks = []
                for h in range(SGU_HEADS):
                    w = jnp.where(causal, sw_ref[0, h], 0.0).astype(BF16)
                    blk = vn[rows, h * SGU_HEAD_DIM:(h + 1) * SGU_HEAD_DIM]
                    blocks.append(_dot(w, blk) + sb_ref[0, h])
                issue_next()
                catbuf[rows, 0:MIX_WIDTH] = u[rows] * jnp.concatenate(blocks, axis=1)
        else:
            bg = zbuf[:, 0:MIX_WIDTH]
            hcur = zbuf[:, MIX_WIDTH:2 * MIX_WIDTH] * zbuf[:, 2 * MIX_WIDTH:3 * MIX_WIDTH]
            zh = _dot(xh_ref[...].astype(BF16), win_ref[0, :, MIX_WIDTH:3 * MIX_WIDTH])
            first_of_seq = (step % cfg.tiles_per_seq) == 0
            hprev = jnp.where(first_of_seq, 0.0, zh[:, :MIX_WIDTH] * zh[:, MIX_WIDTH:])
            rowid = lax.broadcasted_iota(I32, (TM, MIX_WIDTH), 0)
            h1 = jnp.where(rowid == 0, hprev[HALO - 1:HALO], pltpu.roll(hcur, 1, 0))
            h2 = jnp.where(rowid == 0, hprev[HALO - 2:HALO - 1],
                           jnp.where(rowid == 1, hprev[HALO - 1:HALO], pltpu.roll(hcur, 2, 0)))
            cw = cw_ref[0]
            issue_next()
            catbuf[:, 0:MIX_WIDTH] = bg * (cw[0:1] * h2 + cw[1:2] * h1 + cw[2:3] * hcur)

        cat = catbuf[:, 0:D_MODEL].astype(BF16)
        for n in range(D_MODEL // MIXER_SLAB):
            cols = slice(n * MIXER_SLAB, (n + 1) * MIXER_SLAB)
            issue_next()
            obuf[:, cols] = _dot(cat, wout_ref[0, :, cols])
        while issued[0] < TM:
            issue_next()
        y = _layer_norm(ALPHA * x + obuf[...], g1_ref[0], b1_ref[0])

        pos, idx, ga, gb = _route(y, step, rw_ref, rb_ref, tri_ref, low_ref, cnt_ref, cur_ref, nxt_ref, tbl_ref)
        mrow = lax.broadcasted_iota(I32, (LANES, TM), 0)
        meta = jnp.where(mrow == META_IDX, idx,
                         jnp.where(mrow == META_GA, ga, jnp.where(mrow == META_GB, gb, 0.0)))
        meta_t = jnp.transpose(meta)
        posv_ref[pl.ds(pbuf, 1), :] = pos.astype(I32)
        cnt_out[...] = cnt_ref[:, 0:LANES]
        cur_out[...] = cur_ref[:, 0:LANES]
        nxt_out[...] = nxt_ref[:, 0:LANES]
        tbl_out[...] = tbl_ref[...]

        @pl.when(step < 2)
        def _():
            stage_wait(out_stage)

        @pl.when(step >= MIXER_STAGES)
        def _():
            stage_wait(slot)

        own = pl.ds(pl.multiple_of(slot * TM, TM), TM)
        buf_ref[own, 0:D_MODEL] = y
        buf_ref[own, D_MODEL:ROW_W] = meta_t
        pos_copy(pbuf).start()

        @pl.when(step == last)
        def _():
            pos_copy(0).wait()
            pos_copy(1).wait()
            prev_stage = (step + MIXER_STAGES - 1) % MIXER_STAGES

            def issue_last(r, carry):
                stage_copy(prev_stage, r, 1 - pbuf).start()
                stage_copy(slot, r, pbuf).start()
                return carry

            lax.fori_loop(0, TM, issue_last, 0, unroll=8)
            lane = lax.broadcasted_iota(I32, (CLASS_ROWS, LANES), 1).astype(F32)
            crow = lax.broadcasted_iota(I32, (CLASS_ROWS, LANES), 0).astype(F32)
            diag = crow == lane
            cnt_l = jnp.sum(jnp.where(diag, cnt_ref[:, 0:LANES], 0.0), axis=0, keepdims=True)
            cur_l = jnp.sum(jnp.where(diag, cur_ref[:, 0:LANES], 0.0), axis=0, keepdims=True)
            srow = lax.broadcasted_iota(I32, (8, LANES), 0)
            st = jnp.where(srow == 0, cnt_l, jnp.where(srow == 1, cur_l, nxt_ref[:, 0:LANES]))
            stv_ref[...] = st.astype(I32)
            cp = pltpu.make_async_copy(stv_ref, sts_ref, st_sem)
            cp.start()
            cp.wait()
            for stage in range(MIXER_STAGES):
                stage_wait(stage)
            zero_row = pl.multiple_of(out_stage * TM, TM)
            buf_ref[pl.ds(zero_row, TM), 0:ROW_W] = jnp.zeros((TM, ROW_W), F32)
            zrow = buf_ref.at[pl.ds(zero_row, MOE_TILE), pl.ds(0, ROW_W)]
            zsem = scat_sem.at[0]

            def clear_class(c, carry):
                fill = sts_ref[0, c] % MOE_TILE
                start = sts_ref[1, c] * MOE_TILE + fill
                n = jnp.where(fill == 0, 0, MOE_TILE - fill)

                def one(j, carry2):
                    _row_copy(zrow, 0, xs_ref, start + j, zsem).start()
                    return carry2

                def one_wait(j, carry2):
                    _row_copy(zrow, 0, xs_ref, 0, zsem).wait()
                    return carry2

                lax.fori_loop(0, n, one, 0)
                lax.fori_loop(0, n, one_wait, 0)
                return carry

            lax.fori_loop(0, N_CLASSES, clear_class, 0)

            def clear_tile(t, carry):
                start = pl.multiple_of(t * MOE_TILE, MOE_TILE)
                cp2 = pltpu.make_async_copy(zrow, xs_ref.at[pl.ds(start, MOE_TILE)], zsem)
                cp2.start()
                cp2.wait()
                return carry

            lax.fori_loop(sts_ref[2, 0], cfg.n_pool_tiles, clear_tile, 0)

    return body


def _mixer_call(mixer, x, p, layer, j):
    cfg = _cfg()

    def const_spec(shape):
        return pl.BlockSpec(shape, lambda i: (0,) * len(shape))

    def layer_spec(shape, l):
        return pl.BlockSpec((1,) + shape, lambda i: (l,) + (0,) * len(shape))

    x_spec = pl.BlockSpec((TM, D_MODEL), lambda i: (i, 0))
    if mixer == "gmlp":
        head_specs = [
            x_spec,
            layer_spec((D_MODEL, 3 * MIX_WIDTH), j),
            layer_spec((1, MIX_WIDTH), j),
            layer_spec((1, MIX_WIDTH), j),
            layer_spec((SGU_HEADS, SGU_CHUNK, SGU_CHUNK), j),
            layer_spec((SGU_HEADS, SGU_CHUNK, SGU_HEAD_DIM), j),
        ]
        head_args = (x, p["w_in_a"], p["sgu_ln_g"], p["sgu_ln_b"], p["sgu_w"], p["sgu_b_col"])
    else:
        head_specs = [
            x_spec,
            pl.BlockSpec((HALO, D_MODEL), lambda i: (jnp.maximum(i * (TM // HALO) - 1, 0), 0)),
            layer_spec((D_MODEL, 4 * MIX_WIDTH), j),
            layer_spec((CONV_WIDTH, MIX_WIDTH), j),
        ]
        head_args = (x, x, p["w_in_b"], p["conv_w"])
    tail_specs = [
        pl.BlockSpec((1, 1, MIX_WIDTH, N_MEM), lambda i: (layer, i // cfg.tiles_per_seq, 0, 0)),
        pl.BlockSpec((1, 1, N_MEM, MIX_WIDTH), lambda i: (layer, i // cfg.tiles_per_seq, 0, 0)),
        layer_spec((D_MODEL, D_MODEL), layer),
        layer_spec((1, D_MODEL), layer),
        layer_spec((1, D_MODEL), layer),
        const_spec((64, D_MODEL)),
        const_spec((32, 1)),
    ]
    tail_args = (p["kt"], p["v"], p["w_out"], p["ln1_g"], p["ln1_b"], p["rw"], p["rb"])
    return pl.pallas_call(
        _make_mixer_kernel(mixer, cfg),
        grid=(cfg.n_mixer_steps,),
        in_specs=head_specs + tail_specs,
        out_specs=[
            pl.BlockSpec(memory_space=pl.ANY),
            const_spec((CLASS_ROWS, LANES)),
            const_spec((CLASS_ROWS, LANES)),
            const_spec((8, LANES)),
            const_spec((8, TM)),
        ],
        out_shape=[
            jax.ShapeDtypeStruct((cfg.n_pool_rows, ROW_W), F32),
            jax.ShapeDtypeStruct((CLASS_ROWS, LANES), F32),
            jax.ShapeDtypeStruct((CLASS_ROWS, LANES), F32),
            jax.ShapeDtypeStruct((8, LANES), F32),
            jax.ShapeDtypeStruct((8, TM), F32),
        ],
        scratch_shapes=[
            pltpu.VMEM((TM, TM), BF16),
            pltpu.VMEM((CLASS_ROWS, CLASS_ROWS), BF16),
            pltpu.VMEM((CLASS_ROWS, TM), F32),
            pltpu.VMEM((CLASS_ROWS, TM), F32),
            pltpu.VMEM((8, TM), F32),
            pltpu.VMEM((8, TM), F32),
            pltpu.VMEM(((MIXER_STAGES + 2) * TM, 2 * D_MODEL), F32),
            pltpu.VMEM((2, TM), I32),
            pltpu.SMEM((2, TM), I32),
            pltpu.VMEM((8, LANES), I32),
            pltpu.SMEM((8, LANES), I32),
            pltpu.SemaphoreType.DMA((MIXER_STAGES,)),
            pltpu.SemaphoreType.DMA((2,)),
            pltpu.SemaphoreType.DMA(()),
        ],
        compiler_params=pltpu.CompilerParams(dimension_semantics=("arbitrary",), vmem_limit_bytes=VMEM_LIMIT),
        name="mixer_" + mixer,
    )(*head_args, *tail_args)


def _make_moe_kernel(cfg):
    n_slabs = D_MODEL // MOE_SLAB
    rows_per_chunk = MOE_TILE // (4 * n_slabs)
    work_row, facc_row = N_STAGES * MOE_TILE, (N_STAGES + 2) * MOE_TILE

    def body(blk_ref, ea_ref, eb_ref, kind_ref, nvalid_ref, nfull_ref,
             xs_ref, wgu_a_ref, wd_a_ref, wgu_b_ref, wd_b_ref, g2_ref, b2_ref, out_ref,
             buf_ref, idxv_ref, idxs_ref, scat_sem, idx_sem):
        step = pl.program_id(0)
        slot = step % N_STAGES
        prev = (step + N_STAGES - 1) % N_STAGES
        works = [buf_ref.at[pl.ds(work_row + k * MOE_TILE, MOE_TILE)] for k in range(2)]
        facc = buf_ref.at[pl.ds(facc_row, MOE_TILE)]

        ibuf = step % 2
        pbuf = 1 - ibuf

        def stage_copy(stage, r, b):
            tile = buf_ref.at[pl.ds(pl.multiple_of(stage * MOE_TILE, MOE_TILE), MOE_TILE)]
            return _row_copy(tile, r, out_ref, idxs_ref[b * 8, r], scat_sem.at[stage])

        def stage_wait(stage):
            pltpu.make_async_copy(buf_ref.at[pl.ds(0, MOE_TILE)], out_ref.at[pl.ds(0, MOE_TILE)],
                                  scat_sem.at[stage]).wait()

        def experts(issue):
            x = xs_ref[:, 0:D_MODEL]
            xb = x.astype(BF16)
            meta = xs_ref[:, D_MODEL:ROW_W]
            gates = (meta[:, META_GA:META_GA + 1], meta[:, META_GB:META_GB + 1])
            pairs = ((wgu_a_ref, wd_a_ref), (wgu_b_ref, wd_b_ref))
            k_slab = 0
            for k, (wgu_ref, _) in enumerate(pairs):
                for n in range(n_slabs):
                    cols = slice(n * MOE_SLAB, (n + 1) * MOE_SLAB)
                    issue(k_slab)
                    k_slab += 1
                    works[k][:, cols] = _dot(xb, wgu_ref[0, 0, :, cols])
            for k, (_, wd_ref) in enumerate(pairs):
                g = works[k][:, :D_EXPERT]
                hb = (g * _sigmoid(g) * works[k][:, D_EXPERT:]).astype(BF16)
                for n in range(n_slabs):
                    cols = slice(n * MOE_SLAB, (n + 1) * MOE_SLAB)
                    issue(k_slab)
                    k_slab += 1
                    part = gates[k] * _dot(hb, wd_ref[0, 0, :, cols])
                    facc[:, cols] = part if k == 0 else facc[:, cols] + part
            return _layer_norm(ALPHA * x + facc[...], g2_ref[0], b2_ref[0])

        def idx_copy(b):
            dst = idxs_ref.at[pl.ds(pl.multiple_of(b * 8, 8), 8)]
            return pltpu.make_async_copy(idxv_ref, dst, idx_sem.at[b])

        def put_idx(b):
            idx = jnp.transpose(xs_ref[:, D_MODEL:ROW_W])[META_IDX:META_IDX + 1, :]
            idxv_ref[...] = jnp.broadcast_to(idx, idxv_ref.shape).astype(I32)
            return idx_copy(b)

        @pl.when(step == 0)
        def _():
            buf_ref[pl.ds((N_STAGES - 1) * MOE_TILE, MOE_TILE), :] = jnp.zeros((MOE_TILE, D_MODEL), F32)

            def init_idx(r, carry):
                idxs_ref[8, r] = r
                return carry

            lax.fori_loop(0, MOE_TILE, init_idx, 0)

        @pl.when(kind_ref[step] == KIND_FULL)
        def _():
            @pl.when(step > 0)
            def _():
                idx_copy(pbuf).wait()

            @pl.when(step < nfull_ref[0])
            def _():
                put_idx(ibuf).start()

            def issue(chunk):
                for r in range(chunk * rows_per_chunk, (chunk + 1) * rows_per_chunk):
                    stage_copy(prev, r, pbuf).start(priority=r % 2)

            y = experts(issue)

            @pl.when(step == 0)
            def _():
                stage_wait(prev)

            @pl.when(step >= N_STAGES)
            def _():
                stage_wait(slot)

            buf_ref[pl.ds(pl.multiple_of(slot * MOE_TILE, MOE_TILE), MOE_TILE), :] = y

            @pl.when(step == nfull_ref[0])
            def _():
                stage_wait((step + 1) % N_STAGES)
                stage_wait(prev)

        @pl.when(kind_ref[step] == KIND_PARTIAL)
        def _():
            cp = put_idx(0)
            cp.start()
            buf_ref[pl.ds(0, MOE_TILE), :] = experts(lambda chunk: None)
            cp.wait()
            n = nvalid_ref[step]

            def one(r, carry):
                stage_copy(0, r, 0).start()
                return carry

            def one_wait(r, carry):
                _row_copy(buf_ref, 0, out_ref, 0, scat_sem.at[0]).wait()
                return carry

            lax.fori_loop(0, n, one, 0)
            lax.fori_loop(0, n, one_wait, 0)

    return body


def _moe_call(plan, xs, p, layer):
    cfg = _cfg()

    def xmap(i, blk, *_):
        return (blk[i], 0)

    def wa(i, blk, ea, *_):
        return (layer, ea[i], 0, 0)

    def wb(i, blk, ea, eb, *_):
        return (layer, eb[i], 0, 0)

    def lmap(i, *_):
        return (layer, 0, 0)

    return pl.pallas_call(
        _make_moe_kernel(cfg),
        grid_spec=pltpu.PrefetchScalarGridSpec(
            num_scalar_prefetch=6,
            grid=(cfg.n_moe_steps,),
            in_specs=[
                pl.BlockSpec((MOE_TILE, ROW_W), xmap),
                pl.BlockSpec((1, 1, D_MODEL, 2 * D_EXPERT), wa),
                pl.BlockSpec((1, 1, D_EXPERT, D_MODEL), wa),
                pl.BlockSpec((1, 1, D_MODEL, 2 * D_EXPERT), wb),
                pl.BlockSpec((1, 1, D_EXPERT, D_MODEL), wb),
                pl.BlockSpec((1, 1, D_MODEL), lmap),
                pl.BlockSpec((1, 1, D_MODEL), lmap),
            ],
            out_specs=pl.BlockSpec(memory_space=pl.ANY),
            scratch_shapes=[
                pltpu.VMEM(((N_STAGES + 3) * MOE_TILE, D_MODEL), F32),
                pltpu.VMEM((8, MOE_TILE), I32),
                pltpu.SMEM((16, MOE_TILE), I32),
                pltpu.SemaphoreType.DMA((N_STAGES,)),
                pltpu.SemaphoreType.DMA((2,)),
            ],
        ),
        out_shape=jax.ShapeDtypeStruct((cfg.n_tokens, D_MODEL), F32),
        compiler_params=pltpu.CompilerParams(dimension_semantics=("arbitrary",), vmem_limit_bytes=VMEM_LIMIT),
        name="moe_experts",
    )(*plan, xs, p["w_gu"], p["w_down"], p["w_gu"], p["w_down"], p["ln2_g"], p["ln2_b"])


def _moe_plan(cnt_out, cur_out, nxt_out, tbl_out):
    cfg = _cfg()
    cnt = cnt_out[:N_CLASSES, 0].astype(I32)
    cur = cur_out[:N_CLASSES, 0].astype(I32)
    n_used = nxt_out[0, 0].astype(I32)
    tile_cls = jnp.clip(tbl_out[0, :cfg.n_pool_tiles].astype(I32), 0, N_CLASSES - 1)
    tid = jnp.arange(cfg.n_pool_tiles, dtype=I32)
    fill = cnt % MOE_TILE
    ends_here = (cur[None, :] == tid[:, None]) & (fill[None, :] != 0)
    is_partial = jnp.any(ends_here, axis=1)
    n_valid = jnp.where(is_partial, jnp.sum(jnp.where(ends_here, fill[None, :], 0), axis=1), MOE_TILE)
    used = tid < n_used
    big = jnp.int32(1 << 30)
    key = tile_cls * (2 * cfg.n_pool_tiles) + tid
    full = used & ~is_partial
    part = used & is_partial

    def order_of(keys):
        rank = jnp.sum(keys[None, :] < keys[:, None], axis=1)
        return jnp.sum(jnp.where(rank[None, :] == tid[:, None], tid[None, :], 0), axis=1).astype(I32)

    order_full = order_of(jnp.where(full, key, big + tid))
    order_part = order_of(jnp.where(part, key, big + tid))
    n_full = jnp.sum(full).astype(I32)
    n_part = jnp.sum(part).astype(I32)
    last_full = order_full[n_full - 1]

    i = jnp.arange(cfg.n_full_max + 1, dtype=I32)
    blk_a = order_full[jnp.minimum(i, n_full - 1)]
    kind_a = jnp.where(i <= n_full, KIND_FULL, KIND_IDLE)
    k = jnp.arange(N_CLASSES, dtype=I32)
    blk_b = jnp.where(n_part > 0, order_part[jnp.clip(jnp.minimum(k, n_part - 1), 0, None)], last_full)
    kind_b = jnp.where(k < n_part, KIND_PARTIAL, KIND_IDLE)
    blk = jnp.concatenate([blk_a, blk_b]).astype(I32)
    kind = jnp.concatenate([kind_a, kind_b]).astype(I32)
    cls = tile_cls[blk]
    base = (cls // N_PAIRS) * GROUP_SIZE
    ea = base + jnp.asarray(PAIR_LO, I32)[cls % N_PAIRS]
    eb = base + jnp.asarray(PAIR_HI, I32)[cls % N_PAIRS]
    return blk, ea.astype(I32), eb.astype(I32), kind, n_valid[blk].astype(I32), n_full.reshape(1)


def _router_rows(router_w):
    wt = router_w.astype(F32).T.reshape(N_GROUPS, GROUP_SIZE, D_MODEL)
    wt = jnp.transpose(wt, (1, 0, 2))
    wt = jnp.pad(wt, ((0, 0), (0, 8 - N_GROUPS), (0, 0))).reshape(32, D_MODEL)
    hi = wt.astype(BF16)
    lo = (wt - hi.astype(F32)).astype(BF16)
    return jnp.concatenate([hi, lo], axis=0)


def _router_bias_rows(router_b):
    b = jnp.transpose(router_b.astype(F32).reshape(N_GROUPS, GROUP_SIZE), (1, 0))
    return jnp.pad(b, ((0, 0), (0, 8 - N_GROUPS))).reshape(32, 1)


def kernel(x, mem, w_in_a, sgu_ln_g, sgu_ln_b, sgu_w, sgu_b, w_in_b, conv_w, w_kv, w_out, ln1_g, ln1_b,
           router_w, router_b, w_gate, w_up, w_down, ln2_g, ln2_b):
    cfg = _cfg()
    kt, v = _kv_call(mem, w_kv.astype(BF16))
    p = {
        "w_in_a": w_in_a.astype(BF16),
        "sgu_ln_g": sgu_ln_g[:, None, :],
        "sgu_ln_b": sgu_ln_b[:, None, :],
        "sgu_w": sgu_w,
        "sgu_b_col": jnp.broadcast_to(sgu_b[..., None], sgu_b.shape + (SGU_HEAD_DIM,)),
        "w_in_b": w_in_b.astype(BF16),
        "conv_w": conv_w,
        "kt": kt,
        "v": v,
        "w_out": w_out.astype(BF16),
        "ln1_g": ln1_g[:, None, :],
        "ln1_b": ln1_b[:, None, :],
        "rw": _router_rows(router_w),
        "rb": _router_bias_rows(router_b),
        "w_gu": jnp.concatenate([w_gate, w_up], axis=-1).astype(BF16),
        "w_down": w_down.astype(BF16),
        "ln2_g": ln2_g[:, None, :],
        "ln2_b": ln2_b[:, None, :],
    }
    h = x.reshape(cfg.n_tokens, D_MODEL)
    for layer in range(DEPTH):
        mixer = "gmlp" if layer % 2 == 0 else "conv"
        xs, cnt, cur, nxt, tbl = _mixer_call(mixer, h, p, layer, layer // 2)
        h = _moe_call(_moe_plan(cnt, cur, nxt, tbl), xs, p, layer)
    return h.reshape(BATCH, SEQ, D_MODEL)
```

```python
import types

import jax
import jax.numpy as jnp
from jax import lax
from jax.experimental import pallas as pl
from jax.experimental.pallas import tpu as pltpu

D_MODEL = 1024
BATCH = 16
SEQ = 4096
DEPTH = 2
CHUNK = 64
MIX_WIDTH = D_MODEL // 2
SGU_CHUNK = 128
SGU_HEADS = 4
SGU_HEAD_DIM = MIX_WIDTH // SGU_HEADS
CONV_WIDTH = 3
N_MEM = 256
XA_HEADS = 4
XA_HEAD_DIM = MIX_WIDTH // XA_HEADS
N_EXPERTS = 16
N_GROUPS = 4
GROUP_SIZE = N_EXPERTS // N_GROUPS
D_EXPERT = D_MODEL // 2
ALPHA = (2 * DEPTH) ** 0.25
LN_EPS = 1e-5

N_PAIRS = GROUP_SIZE * (GROUP_SIZE - 1) // 2
N_CLASSES = N_GROUPS * N_PAIRS
CLASS_ROWS = 32
PAIR_LO = (0, 0, 0, 1, 1, 2)
PAIR_HI = (1, 2, 3, 2, 3, 3)

LANES = 128
TM = 512
MOE_TILE = 256
ROW_W = D_MODEL + LANES
META_IDX, META_GA, META_GB = 0, 1, 2
HALO = 8
MIXER_STAGES = 4
N_STAGES = 3
MIXER_SLAB = 256
MIXER_SPLIT = 2
MIXER_ISSUE_ROWS = 16
MOE_SLAB = 256
VMEM_LIMIT = 56 * 1024 * 1024
KIND_FULL, KIND_PARTIAL, KIND_IDLE = 0, 1, 2

F32 = jnp.float32
BF16 = jnp.bfloat16
I32 = jnp.int32


def _cfg():
    n_tokens = BATCH * SEQ
    n_full_max = n_tokens // MOE_TILE
    n_pool_tiles = n_full_max + N_CLASSES
    assert n_pool_tiles <= TM and SEQ % TM == 0 and TM % SGU_CHUNK == 0
    assert n_tokens // TM >= MIXER_STAGES and n_tokens - N_CLASSES * (MOE_TILE - 1) >= N_STAGES * MOE_TILE
    return types.SimpleNamespace(
        n_tokens=n_tokens, n_mixer_steps=n_tokens // TM, tiles_per_seq=SEQ // TM,
        n_full_max=n_full_max, n_pool_tiles=n_pool_tiles, n_pool_rows=n_pool_tiles * MOE_TILE,
        n_moe_steps=n_full_max + 1 + N_CLASSES)


def _dot(a, b):
    return jnp.dot(a, b, preferred_element_type=F32)


def _dot_nt(a, b):
    return lax.dot_general(a, b, (((1,), (1,)), ((), ())), preferred_element_type=F32)


def _layer_norm(x, g, b):
    mu = jnp.mean(x, axis=-1, keepdims=True)
    xc = x - mu
    var = jnp.mean(xc * xc, axis=-1, keepdims=True)
    return xc * lax.rsqrt(var + LN_EPS) * g + b


def _sigmoid(x):
    return 1.0 / (1.0 + jnp.exp(-x))


def _row_copy(src_ref, src_row, dst_ref, dst_row, sem):
    return pltpu.make_async_copy(src_ref.at[pl.ds(src_row, 1)], dst_ref.at[pl.ds(dst_row, 1)], sem)


def _kv_kernel(mem_ref, wkv_ref, kt_ref, v_ref):
    kv = _dot(mem_ref[0].astype(BF16), wkv_ref[0])
    kt_ref[0, 0] = kv[:, :MIX_WIDTH].T.astype(BF16)
    v_ref[0, 0] = kv[:, MIX_WIDTH:].astype(BF16)


def _kv_call(mem, wkv_bf16):
    return pl.pallas_call(
        _kv_kernel,
        grid=(DEPTH, BATCH),
        in_specs=[
            pl.BlockSpec((1, N_MEM, D_MODEL), lambda l, b: (b, 0, 0)),
            pl.BlockSpec((1, D_MODEL, 2 * MIX_WIDTH), lambda l, b: (l, 0, 0)),
        ],
        out_specs=[
            pl.BlockSpec((1, 1, MIX_WIDTH, N_MEM), lambda l, b: (l, b, 0, 0)),
            pl.BlockSpec((1, 1, N_MEM, MIX_WIDTH), lambda l, b: (l, b, 0, 0)),
        ],
        out_shape=[
            jax.ShapeDtypeStruct((DEPTH, BATCH, MIX_WIDTH, N_MEM), BF16),
            jax.ShapeDtypeStruct((DEPTH, BATCH, N_MEM, MIX_WIDTH), BF16),
        ],
        name="kv_proj",
    )(mem, wkv_bf16)


def _route(y, step, rw_ref, rb_ref, tri_ref, low_ref, cnt_ref, cur_ref, nxt_ref, tbl_ref):
    y_hi = y.astype(BF16)
    y_lo = (y - y_hi.astype(F32)).astype(BF16)
    l_hi = _dot_nt(rw_ref[...], y_hi)
    l_lo = _dot_nt(rw_ref[0:32, :], y_lo)
    logits = l_hi[0:32] + l_hi[32:64] + l_lo
    scores = _sigmoid(logits)
    sel = scores + rb_ref[...]

    row = lax.broadcasted_iota(I32, (8, TM), 0).astype(F32)
    s = [sel[8 * m:8 * m + 8] for m in range(GROUP_SIZE)]
    sc = [scores[8 * m:8 * m + 8] for m in range(GROUP_SIZE)]
    neg = jnp.float32(-jnp.inf)
    m1 = jnp.maximum(jnp.maximum(s[0], s[1]), jnp.maximum(s[2], s[3]))
    i1 = jnp.where(s[0] == m1, 0.0, jnp.where(s[1] == m1, 1.0, jnp.where(s[2] == m1, 2.0, 3.0)))
    t = [jnp.where(i1 == float(m), neg, s[m]) for m in range(GROUP_SIZE)]
    m2 = jnp.maximum(jnp.maximum(t[0], t[1]), jnp.maximum(t[2], t[3]))
    i2 = jnp.where(t[0] == m2, 0.0, jnp.where(t[1] == m2, 1.0, jnp.where(t[2] == m2, 2.0, 3.0)))
    gscore = jnp.where(row < float(N_GROUPS), m1 + m2, neg)
    gmax = jnp.max(gscore, axis=0, keepdims=True)
    top_group = jnp.min(jnp.where(gscore == gmax, row, 8.0), axis=0, keepdims=True)
    lo = jnp.minimum(i1, i2)
    hi = jnp.maximum(i1, i2)
    pair = jnp.where(lo == 0.0, 0.0, jnp.where(lo == 1.0, 3.0, 5.0)) + hi - lo - 1.0
    s_lo = sum(jnp.where(lo == float(m), sc[m], 0.0) for m in range(GROUP_SIZE))
    s_hi = sum(jnp.where(hi == float(m), sc[m], 0.0) for m in range(GROUP_SIZE))
    inv = 1.0 / (s_lo + s_hi)
    chosen = row == top_group
    cls = jnp.sum(jnp.where(chosen, row * float(N_PAIRS) + pair, 0.0), axis=0, keepdims=True)
    ga = jnp.sum(jnp.where(chosen, s_lo * inv, 0.0), axis=0, keepdims=True)
    gb = jnp.sum(jnp.where(chosen, s_hi * inv, 0.0), axis=0, keepdims=True)

    crow = lax.broadcasted_iota(I32, (CLASS_ROWS, TM), 0).astype(F32)
    onehot = jnp.where(crow == cls, 1.0, 0.0)
    before = _dot(onehot.astype(BF16), tri_ref[...])
    cnt = cnt_ref[...]
    cur = cur_ref[...]
    nxt = nxt_ref[0:1, :]
    tile_f = float(MOE_TILE)
    inv_tile = 1.0 / MOE_TILE
    rank = cnt + before
    cnt_new = cnt + jnp.sum(onehot, axis=1, keepdims=True)
    tiles_old = jnp.floor((cnt + (tile_f - 1.0)) * inv_tile)
    tiles_new = jnp.floor((cnt_new + (tile_f - 1.0)) * inv_tile)
    fresh = tiles_new - tiles_old
    base = nxt + _dot(low_ref[...], fresh.astype(BF16))
    q = jnp.floor(rank * inv_tile)
    tile_of = jnp.where(q < tiles_old, cur, base + (q - tiles_old))
    pos_all = tile_of * tile_f + (rank - q * tile_f)
    pos = jnp.sum(onehot * pos_all, axis=0, keepdims=True)

    cnt_ref[...] = cnt_new
    cur_ref[...] = jnp.where(fresh > 0.0, base + fresh - 1.0, cur)
    nxt_ref[...] = jnp.broadcast_to(nxt + jnp.sum(fresh, axis=0, keepdims=True), nxt_ref.shape)
    ids = lax.broadcasted_iota(I32, (CLASS_ROWS, TM), 1).astype(F32)
    hit = jnp.where((ids >= base) & (ids < base + fresh), 1.0, 0.0)
    cls_hit = jnp.sum(hit * crow, axis=0, keepdims=True)
    any_hit = jnp.sum(hit, axis=0, keepdims=True)
    tbl_ref[...] = jnp.where(any_hit > 0.0, cls_hit, tbl_ref[...])

    idx = (lax.broadcasted_iota(I32, (1, TM), 1) + step * TM).astype(F32)
    return pos, idx, ga, gb


def _make_mixer_kernel(mixer, cfg):
    def body(*refs):
        if mixer == "gmlp":
            (x_ref, win_ref, lng_ref, lnb_ref, sw_ref, sb_ref, *rest) = refs
        else:
            (x_ref, xh_ref, win_ref, cw_ref, *rest) = refs
        (kt_ref, v_ref, wout_ref, g1_ref, b1_ref, rw_ref, rb_ref,
         xs_ref, cnt_out, cur_out, nxt_out, tbl_out,
         tri_ref, low_ref, cnt_ref, cur_ref, nxt_ref, tbl_ref, buf_ref, posv_ref, poss_ref,
         stv_ref, sts_ref, scat_sem, pos_sem, st_sem) = rest

        step = pl.program_id(0)
        last = pl.num_programs(0) - 1
        slot = step % MIXER_STAGES
        out_stage = (step + MIXER_STAGES - 2) % MIXER_STAGES
        pbuf = step % 2
        zbuf = buf_ref.at[pl.ds(MIXER_STAGES * TM, TM)]
        catbuf = buf_ref.at[pl.ds((MIXER_STAGES + 1) * TM, TM), pl.ds(0, D_MODEL)]
        obuf = buf_ref.at[pl.ds((MIXER_STAGES + 1) * TM, TM), pl.ds(D_MODEL, D_MODEL)]

        def stage_copy(stage, r, b):
            tile = buf_ref.at[pl.ds(pl.multiple_of(stage * TM, TM), TM), pl.ds(0, ROW_W)]
            return pltpu.make_async_copy(tile.at[pl.ds(r, 1)], xs_ref.at[pl.ds(poss_ref[b, r], 1)],
                                         scat_sem.at[stage])

        def pos_copy(b):
            return pltpu.make_async_copy(posv_ref.at[pl.ds(b, 1)], poss_ref.at[pl.ds(b, 1)], pos_sem.at[b])

        def stage_wait(stage):
            pltpu.make_async_copy(buf_ref.at[pl.ds(0, TM), pl.ds(0, ROW_W)], xs_ref.at[pl.ds(0, TM)],
                                  scat_sem.at[stage]).wait()

        @pl.when(step == 0)
        def _():
            r = lax.broadcasted_iota(I32, (TM, TM), 0)
            c = lax.broadcasted_iota(I32, (TM, TM), 1)
            tri_ref[...] = jnp.where(r < c, 1.0, 0.0).astype(BF16)
            r = lax.broadcasted_iota(I32, (CLASS_ROWS, CLASS_ROWS), 0)
            c = lax.broadcasted_iota(I32, (CLASS_ROWS, CLASS_ROWS), 1)
            low_ref[...] = jnp.where(c < r, 1.0, 0.0).astype(BF16)
            cnt_ref[...] = jnp.zeros_like(cnt_ref)
            cur_ref[...] = jnp.full_like(cur_ref, -1.0)
            nxt_ref[...] = jnp.zeros_like(nxt_ref)
            tbl_ref[...] = jnp.full_like(tbl_ref, -1.0)
            buf_ref[pl.ds((MIXER_STAGES - 2) * TM, 2 * TM), 0:ROW_W] = jnp.zeros((2 * TM, ROW_W), F32)

            def init_pos(r, carry):
                poss_ref[0, r] = r
                poss_ref[1, r] = r
                return carry

            lax.fori_loop(0, TM, init_pos, 0)

        @pl.when(step >= 2)
        def _():
            pos_copy(pbuf).wait()

        issued = [0]

        def issue_next():
            lo = issued[0]
            hi = min(lo + MIXER_ISSUE_ROWS, TM)
            for r in range(lo, hi):
                stage_copy(out_stage, r, pbuf).start(priority=r % 2)
            issued[0] = hi

        hrows = TM // MIXER_SPLIT
        n_in_slabs = win_ref.shape[-1] // MIXER_SLAB
        q_slabs = MIX_WIDTH // MIXER_SLAB
        q_lo = win_ref.shape[-1] - MIX_WIDTH
        scale = XA_HEAD_DIM ** -0.5
        xg, conv_tail = {}, {}

        def in_proj(g):
            rows = slice(g * hrows, (g + 1) * hrows)
            xg[g] = x_ref[rows, :]
            xb = xg[g].astype(BF16)
            for n in list(range(n_in_slabs - q_slabs, n_in_slabs)) + list(range(n_in_slabs - q_slabs)):
                cols = slice(n * MIXER_SLAB, (n + 1) * MIXER_SLAB)
                issue_next()
                zbuf[rows, cols] = _dot(xb, win_ref[0, :, cols])

        def attention(g):
            rows = slice(g * hrows, (g + 1) * hrows)
            for h in range(XA_HEADS):
                lo, hi = h * XA_HEAD_DIM, (h + 1) * XA_HEAD_DIM
                qh = zbuf[rows, q_lo + lo:q_lo + hi].astype(BF16)
                sc = _dot(qh, kt_ref[0, 0, lo:hi, :]) * scale
                e = jnp.exp(sc - jnp.max(sc, axis=-1, keepdims=True))
                p = e * (1.0 / jnp.sum(e, axis=-1, keepdims=True))
                issue_next()
                catbuf[rows, MIX_WIDTH + lo:MIX_WIDTH + hi] = _dot(p.astype(BF16), v_ref[0, 0, :, lo:hi])

        def token_mixer(g):
            rows = slice(g * hrows, (g + 1) * hrows)
            if mixer == "gmlp":
                zuv = zbuf[rows, 0:2 * MIX_WIDTH]
                uv = zuv * (lax.erf(zuv * (2.0 ** -0.5)) + 1.0) * 0.5
                u = uv[:, :MIX_WIDTH]
                vn = _layer_norm(uv[:, MIX_WIDTH:], lng_ref[0], lnb_ref[0]).astype(BF16)
                pos_r = lax.broadcasted_iota(I32, (SGU_CHUNK, SGU_CHUNK), 0) // CHUNK
                pos_c = lax.broadcasted_iota(I32, (SGU_CHUNK, SGU_CHUNK), 1) // CHUNK
                causal = pos_r >= pos_c
                for c in range(hrows // SGU_CHUNK):
                    crows = slice(c * SGU_CHUNK, (c + 1) * SGU_CHUNK)
                    blocks = []
                    for h in range(SGU_HEADS):
                        w = jnp.where(causal, sw_ref[0, h], 0.0).astype(BF16)
                        blk = vn[crows, h * SGU_HEAD_DIM:(h + 1) * SGU_HEAD_DIM]
                        blocks.append(_dot(w, blk) + sb_ref[0, h])
                    issue_next()
                    dst = slice(rows.start + crows.start, rows.start + crows.stop)
                    catbuf[dst, 0:MIX_WIDTH] = u[crows] * jnp.concatenate(blocks, axis=1)
            else:
                if g == 0:
                    zh = _dot(xh_ref[...].astype(BF16), win_ref[0, :, MIX_WIDTH:3 * MIX_WIDTH])
                    first_of_seq = (step % cfg.tiles_per_seq) == 0
                    hprev = jnp.where(first_of_seq, 0.0, zh[:, :MIX_WIDTH] * zh[:, MIX_WIDTH:])
                else:
                    hprev = conv_tail[g - 1]
                bg = zbuf[rows, 0:MIX_WIDTH]
                hcur = zbuf[rows, MIX_WIDTH:2 * MIX_WIDTH] * zbuf[rows, 2 * MIX_WIDTH:3 * MIX_WIDTH]
                conv_tail[g] = hcur[hrows - HALO:hrows]
                rowid = lax.broadcasted_iota(I32, (hrows, MIX_WIDTH), 0)
                h1 = jnp.where(rowid == 0, hprev[HALO - 1:HALO], pltpu.roll(hcur, 1, 0))
                h2 = jnp.where(rowid == 0, hprev[HALO - 2:HALO - 1],
                               jnp.where(rowid == 1, hprev[HALO - 1:HALO], pltpu.roll(hcur, 2, 0)))
                cw = cw_ref[0]
                issue_next()
                catbuf[rows, 0:MIX_WIDTH] = bg * (cw[0:1] * h2 + cw[1:2] * h1 + cw[2:3] * hcur)

        def out_proj(g):
            rows = slice(g * hrows, (g + 1) * hrows)
            cat = catbuf[rows, 0:D_MODEL].astype(BF16)
            for n in range(D_MODEL // MIXER_SLAB):
                cols = slice(n * MIXER_SLAB, (n + 1) * MIXER_SLAB)
                issue_next()
                obuf[rows, cols] = _dot(cat, wout_ref[0, :, cols])

        assert MIXER_SPLIT == 2
        in_proj(0)
        in_proj(1)
        attention(0)
        token_mixer(0)
        attention(1)
        out_proj(0)
        token_mixer(1)
        out_proj(1)
        while issued[0] < TM:
            issue_next()
        y = jnp.concatenate(
            [_layer_norm(ALPHA * xg[g] + obuf[g * hrows:(g + 1) * hrows, :], g1_ref[0], b1_ref[0])
             for g in range(MIXER_SPLIT)], axis=0)

        pos, idx, ga, gb = _route(y, step, rw_ref, rb_ref, tri_ref, low_ref, cnt_ref, cur_ref, nxt_ref, tbl_ref)
        mrow = lax.broadcasted_iota(I32, (LANES, TM), 0)
        meta = jnp.where(mrow == META_IDX, idx,
                         jnp.where(mrow == META_GA, ga, jnp.where(mrow == META_GB, gb, 0.0)))
        meta_t = jnp.transpose(meta)
        posv_ref[pl.ds(pbuf, 1), :] = pos.astype(I32)
        cnt_out[...] = cnt_ref[:, 0:LANES]
        cur_out[...] = cur_ref[:, 0:LANES]
        nxt_out[...] = nxt_ref[:, 0:LANES]
        tbl_out[...] = tbl_ref[...]

        @pl.when(step < 2)
        def _():
            stage_wait(out_stage)

        @pl.when(step >= MIXER_STAGES)
        def _():
            stage_wait(slot)

        own = pl.ds(pl.multiple_of(slot * TM, TM), TM)
        buf_ref[own, 0:D_MODEL] = y
        buf_ref[own, D_MODEL:ROW_W] = meta_t
        pos_copy(pbuf).start()

        @pl.when(step == last)
        def _():
            pos_copy(0).wait()
            pos_copy(1).wait()
            prev_stage = (step + MIXER_STAGES - 1) % MIXER_STAGES

            def issue_last(r, carry):
                stage_copy(prev_stage, r, 1 - pbuf).start()
                stage_copy(slot, r, pbuf).start()
                return carry

            lax.fori_loop(0, TM, issue_last, 0, unroll=8)
            lane = lax.broadcasted_iota(I32, (CLASS_ROWS, LANES), 1).astype(F32)
            crow = lax.broadcasted_iota(I32, (CLASS_ROWS, LANES), 0).astype(F32)
            diag = crow == lane
            cnt_l = jnp.sum(jnp.where(diag, cnt_ref[:, 0:LANES], 0.0), axis=0, keepdims=True)
            cur_l = jnp.sum(jnp.where(diag, cur_ref[:, 0:LANES], 0.0), axis=0, keepdims=True)
            srow = lax.broadcasted_iota(I32, (8, LANES), 0)
            st = jnp.where(srow == 0, cnt_l, jnp.where(srow == 1, cur_l, nxt_ref[:, 0:LANES]))
            stv_ref[...] = st.astype(I32)
            cp = pltpu.make_async_copy(stv_ref, sts_ref, st_sem)
            cp.start()
            cp.wait()
            for stage in range(MIXER_STAGES):
                stage_wait(stage)
            zero_row = pl.multiple_of(out_stage * TM, TM)
            buf_ref[pl.ds(zero_row, TM), 0:ROW_W] = jnp.zeros((TM, ROW_W), F32)
            zrow = buf_ref.at[pl.ds(zero_row, MOE_TILE), pl.ds(0, ROW_W)]
            zsem = scat_sem.at[0]

            def clear_class(c, carry):
                fill = sts_ref[0, c] % MOE_TILE
                start = sts_ref[1, c] * MOE_TILE + fill
                n = jnp.where(fill == 0, 0, MOE_TILE - fill)

                def one(j, carry2):
                    _row_copy(zrow, 0, xs_ref, start + j, zsem).start()
                    return carry2

                def one_wait(j, carry2):
                    _row_copy(zrow, 0, xs_ref, 0, zsem).wait()
                    return carry2

                lax.fori_loop(0, n, one, 0)
                lax.fori_loop(0, n, one_wait, 0)
                return carry

            lax.fori_loop(0, N_CLASSES, clear_class, 0)

            def clear_tile(t, carry):
                start = pl.multiple_of(t * MOE_TILE, MOE_TILE)
                cp2 = pltpu.make_async_copy(zrow, xs_ref.at[pl.ds(start, MOE_TILE)], zsem)
                cp2.start()
                cp2.wait()
                return carry

            lax.fori_loop(sts_ref[2, 0], cfg.n_pool_tiles, clear_tile, 0)

    return body


def _mixer_call(mixer, x, p, layer, j):
    cfg = _cfg()

    def const_spec(shape):
        return pl.BlockSpec(shape, lambda i: (0,) * len(shape))

    def layer_spec(shape, l):
        return pl.BlockSpec((1,) + shape, lambda i: (l,) + (0,) * len(shape))

    x_spec = pl.BlockSpec((TM, D_MODEL), lambda i: (i, 0))
    if mixer == "gmlp":
        head_specs = [
            x_spec,
            layer_spec((D_MODEL, 3 * MIX_WIDTH), j),
            layer_spec((1, MIX_WIDTH), j),
            layer_spec((1, MIX_WIDTH), j),
            layer_spec((SGU_HEADS, SGU_CHUNK, SGU_CHUNK), j),
            layer_spec((SGU_HEADS, SGU_CHUNK, SGU_HEAD_DIM), j),
        ]
        head_args = (x, p["w_in_a"], p["sgu_ln_g"], p["sgu_ln_b"], p["sgu_w"], p["sgu_b_col"])
    else:
        head_specs = [
            x_spec,
            pl.BlockSpec((HALO, D_MODEL), lambda i: (jnp.maximum(i * (TM // HALO) - 1, 0), 0)),
            layer_spec((D_MODEL, 4 * MIX_WIDTH), j),
            layer_spec((CONV_WIDTH, MIX_WIDTH), j),
        ]
        head_args = (x, x, p["w_in_b"], p["conv_w"])
    tail_specs = [
        pl.BlockSpec((1, 1, MIX_WIDTH, N_MEM), lambda i: (layer, i // cfg.tiles_per_seq, 0, 0)),
        pl.BlockSpec((1, 1, N_MEM, MIX_WIDTH), lambda i: (layer, i // cfg.tiles_per_seq, 0, 0)),
        layer_spec((D_MODEL, D_MODEL), layer),
        layer_spec((1, D_MODEL), layer),
        layer_spec((1, D_MODEL), layer),
        const_spec((64, D_MODEL)),
        const_spec((32, 1)),
    ]
    tail_args = (p["kt"], p["v"], p["w_out"], p["ln1_g"], p["ln1_b"], p["rw"], p["rb"])
    return pl.pallas_call(
        _make_mixer_kernel(mixer, cfg),
        grid=(cfg.n_mixer_steps,),
        in_specs=head_specs + tail_specs,
        out_specs=[
            pl.BlockSpec(memory_space=pl.ANY),
            const_spec((CLASS_ROWS, LANES)),
            const_spec((CLASS_ROWS, LANES)),
            const_spec((8, LANES)),
            const_spec((8, TM)),
        ],
        out_shape=[
            jax.ShapeDtypeStruct((cfg.n_pool_rows, ROW_W), F32),
            jax.ShapeDtypeStruct((CLASS_ROWS, LANES), F32),
            jax.ShapeDtypeStruct((CLASS_ROWS, LANES), F32),
            jax.ShapeDtypeStruct((8, LANES), F32),
            jax.ShapeDtypeStruct((8, TM), F32),
        ],
        scratch_shapes=[
            pltpu.VMEM((TM, TM), BF16),
            pltpu.VMEM((CLASS_ROWS, CLASS_ROWS), BF16),
            pltpu.VMEM((CLASS_ROWS, TM), F32),
            pltpu.VMEM((CLASS_ROWS, TM), F32),
            pltpu.VMEM((8, TM), F32),
            pltpu.VMEM((8, TM), F32),
            pltpu.VMEM(((MIXER_STAGES + 2) * TM, 2 * D_MODEL), F32),
            pltpu.VMEM((2, TM), I32),
            pltpu.SMEM((2, TM), I32),
            pltpu.VMEM((8, LANES), I32),
            pltpu.SMEM((8, LANES), I32),
            pltpu.SemaphoreType.DMA((MIXER_STAGES,)),
            pltpu.SemaphoreType.DMA((2,)),
            pltpu.SemaphoreType.DMA(()),
        ],
        compiler_params=pltpu.CompilerParams(dimension_semantics=("arbitrary",), vmem_limit_bytes=VMEM_LIMIT),
        name="mixer_" + mixer,
    )(*head_args, *tail_args)


def _make_moe_kernel(cfg):
    n_slabs = D_MODEL // MOE_SLAB
    rows_per_chunk = MOE_TILE // (4 * n_slabs)
    work_row, facc_row = N_STAGES * MOE_TILE, (N_STAGES + 2) * MOE_TILE

    def body(blk_ref, ea_ref, eb_ref, kind_ref, nvalid_ref, nfull_ref,
             xs_ref, wgu_a_ref, wd_a_ref, wgu_b_ref, wd_b_ref, g2_ref, b2_ref, out_ref,
             buf_ref, idxv_ref, idxs_ref, scat_sem, idx_sem):
        step = pl.program_id(0)
        slot = step % N_STAGES
        prev = (step + N_STAGES - 1) % N_STAGES
        works = [buf_ref.at[pl.ds(work_row + k * MOE_TILE, MOE_TILE)] for k in range(2)]
        facc = buf_ref.at[pl.ds(facc_row, MOE_TILE)]

        ibuf = step % 2
        pbuf = 1 - ibuf

        def stage_copy(stage, r, b):
            tile = buf_ref.at[pl.ds(pl.multiple_of(stage * MOE_TILE, MOE_TILE), MOE_TILE)]
            return _row_copy(tile, r, out_ref, idxs_ref[b * 8, r], scat_sem.at[stage])

        def stage_wait(stage):
            pltpu.make_async_copy(buf_ref.at[pl.ds(0, MOE_TILE)], out_ref.at[pl.ds(0, MOE_TILE)],
                                  scat_sem.at[stage]).wait()

        def experts(issue):
            x = xs_ref[:, 0:D_MODEL]
            xb = x.astype(BF16)
            meta = xs_ref[:, D_MODEL:ROW_W]
            gates = (meta[:, META_GA:META_GA + 1], meta[:, META_GB:META_GB + 1])
            pairs = ((wgu_a_ref, wd_a_ref), (wgu_b_ref, wd_b_ref))
            k_slab = 0
            for k, (wgu_ref, _) in enumerate(pairs):
                for n in range(n_slabs):
                    cols = slice(n * MOE_SLAB, (n + 1) * MOE_SLAB)
                    issue(k_slab)
                    k_slab += 1
                    works[k][:, cols] = _dot(xb, wgu_ref[0, 0, :, cols])
            for k, (_, wd_ref) in enumerate(pairs):
                g = works[k][:, :D_EXPERT]
                hb = (g * _sigmoid(g) * works[k][:, D_EXPERT:]).astype(BF16)
                for n in range(n_slabs):
                    cols = slice(n * MOE_SLAB, (n + 1) * MOE_SLAB)
                    issue(k_slab)
                    k_slab += 1
                    part = gates[k] * _dot(hb, wd_ref[0, 0, :, cols])
                    facc[:, cols] = part if k == 0 else facc[:, cols] + part
            return _layer_norm(ALPHA * x + facc[...], g2_ref[0], b2_ref[0])

        def idx_copy(b):
            dst = idxs_ref.at[pl.ds(pl.multiple_of(b * 8, 8), 8)]
            return pltpu.make_async_copy(idxv_ref, dst, idx_sem.at[b])

        def put_idx(b):
            idx = jnp.transpose(xs_ref[:, D_MODEL:ROW_W])[META_IDX:META_IDX + 1, :]
            idxv_ref[...] = jnp.broadcast_to(idx, idxv_ref.shape).astype(I32)
            return idx_copy(b)

        @pl.when(step == 0)
        def _():
            buf_ref[pl.ds((N_STAGES - 1) * MOE_TILE, MOE_TILE), :] = jnp.zeros((MOE_TILE, D_MODEL), F32)

            def init_idx(r, carry):
                idxs_ref[8, r] = r
                return carry

            lax.fori_loop(0, MOE_TILE, init_idx, 0)

        @pl.when(kind_ref[step] == KIND_FULL)
        def _():
            @pl.when(step > 0)
            def _():
                idx_copy(pbuf).wait()

            @pl.when(step < nfull_ref[0])
            def _():
                put_idx(ibuf).start()

            def issue(chunk):
                for r in range(chunk * rows_per_chunk, (chunk + 1) * rows_per_chunk):
                    stage_copy(prev, r, pbuf).start(priority=r % 2)

            y = experts(issue)

            @pl.when(step == 0)
            def _():
                stage_wait(prev)

            @pl.when(step >= N_STAGES)
            def _():
                stage_wait(slot)

            buf_ref[pl.ds(pl.multiple_of(slot * MOE_TILE, MOE_TILE), MOE_TILE), :] = y

            @pl.when(step == nfull_ref[0])
            def _():
                stage_wait((step + 1) % N_STAGES)
                stage_wait(prev)

        @pl.when(kind_ref[step] == KIND_PARTIAL)
        def _():
            cp = put_idx(0)
            cp.start()
            buf_ref[pl.ds(0, MOE_TILE), :] = experts(lambda chunk: None)
            cp.wait()
            n = nvalid_ref[step]

            def one(r, carry):
                stage_copy(0, r, 0).start()
                return carry

            def one_wait(r, carry):
                _row_copy(buf_ref, 0, out_ref, 0, scat_sem.at[0]).wait()
                return carry

            lax.fori_loop(0, n, one, 0)
            lax.fori_loop(0, n, one_wait, 0)

    return body


def _moe_call(plan, xs, p, layer):
    cfg = _cfg()

    def xmap(i, blk, *_):
        return (blk[i], 0)

    def wa(i, blk, ea, *_):
        return (layer, ea[i], 0, 0)

    def wb(i, blk, ea, eb, *_):
        return (layer, eb[i], 0, 0)

    def lmap(i, *_):
        return (layer, 0, 0)

    return pl.pallas_call(
        _make_moe_kernel(cfg),
        grid_spec=pltpu.PrefetchScalarGridSpec(
            num_scalar_prefetch=6,
            grid=(cfg.n_moe_steps,),
            in_specs=[
                pl.BlockSpec((MOE_TILE, ROW_W), xmap),
                pl.BlockSpec((1, 1, D_MODEL, 2 * D_EXPERT), wa),
                pl.BlockSpec((1, 1, D_EXPERT, D_MODEL), wa),
                pl.BlockSpec((1, 1, D_MODEL, 2 * D_EXPERT), wb),
                pl.BlockSpec((1, 1, D_EXPERT, D_MODEL), wb),
                pl.BlockSpec((1, 1, D_MODEL), lmap),
                pl.BlockSpec((1, 1, D_MODEL), lmap),
            ],
            out_specs=pl.BlockSpec(memory_space=pl.ANY),
            scratch_shapes=[
                pltpu.VMEM(((N_STAGES + 3) * MOE_TILE, D_MODEL), F32),
                pltpu.VMEM((8, MOE_TILE), I32),
                pltpu.SMEM((16, MOE_TILE), I32),
                pltpu.SemaphoreType.DMA((N_STAGES,)),
                pltpu.SemaphoreType.DMA((2,)),
            ],
        ),
        out_shape=jax.ShapeDtypeStruct((cfg.n_tokens, D_MODEL), F32),
        compiler_params=pltpu.CompilerParams(dimension_semantics=("arbitrary",), vmem_limit_bytes=VMEM_LIMIT),
        name="moe_experts",
    )(*plan, xs, p["w_gu"], p["w_down"], p["w_gu"], p["w_down"], p["ln2_g"], p["ln2_b"])


def _moe_plan(cnt_out, cur_out, nxt_out, tbl_out):
    cfg = _cfg()
    cnt = cnt_out[:N_CLASSES, 0].astype(I32)
    cur = cur_out[:N_CLASSES, 0].astype(I32)
    n_used = nxt_out[0, 0].astype(I32)
    tile_cls = jnp.clip(tbl_out[0, :cfg.n_pool_tiles].astype(I32), 0, N_CLASSES - 1)
    tid = jnp.arange(cfg.n_pool_tiles, dtype=I32)
    fill = cnt % MOE_TILE
    ends_here = (cur[None, :] == tid[:, None]) & (fill[None, :] != 0)
    is_partial = jnp.any(ends_here, axis=1)
    n_valid = jnp.where(is_partial, jnp.sum(jnp.where(ends_here, fill[None, :], 0), axis=1), MOE_TILE)
    used = tid < n_used
    big = jnp.int32(1 << 30)
    key = tile_cls * (2 * cfg.n_pool_tiles) + tid
    full = used & ~is_partial
    part = used & is_partial

    def order_of(keys):
        rank = jnp.sum(keys[None, :] < keys[:, None], axis=1)
        return jnp.sum(jnp.where(rank[None, :] == tid[:, None], tid[None, :], 0), axis=1).astype(I32)

    order_full = order_of(jnp.where(full, key, big + tid))
    order_part = order_of(jnp.where(part, key, big + tid))
    n_full = jnp.sum(full).astype(I32)
    n_part = jnp.sum(part).astype(I32)
    last_full = order_full[n_full - 1]

    i = jnp.arange(cfg.n_full_max + 1, dtype=I32)
    blk_a = order_full[jnp.minimum(i, n_full - 1)]
    kind_a = jnp.where(i <= n_full, KIND_FULL, KIND_IDLE)
    k = jnp.arange(N_CLASSES, dtype=I32)
    blk_b = jnp.where(n_part > 0, order_part[jnp.clip(jnp.minimum(k, n_part - 1), 0, None)], last_full)
    kind_b = jnp.where(k < n_part, KIND_PARTIAL, KIND_IDLE)
    blk = jnp.concatenate([blk_a, blk_b]).astype(I32)
    kind = jnp.concatenate([kind_a, kind_b]).astype(I32)
    cls = tile_cls[blk]
    base = (cls // N_PAIRS) * GROUP_SIZE
    ea = base + jnp.asarray(PAIR_LO, I32)[cls % N_PAIRS]
    eb = base + jnp.asarray(PAIR_HI, I32)[cls % N_PAIRS]
    return blk, ea.astype(I32), eb.astype(I32), kind, n_valid[blk].astype(I32), n_full.reshape(1)


def _router_rows(router_w):
    wt = router_w.astype(F32).T.reshape(N_GROUPS, GROUP_SIZE, D_MODEL)
    wt = jnp.transpose(wt, (1, 0, 2))
    wt = jnp.pad(wt, ((0, 0), (0, 8 - N_GROUPS), (0, 0))).reshape(32, D_MODEL)
    hi = wt.astype(BF16)
    lo = (wt - hi.astype(F32)).astype(BF16)
    return jnp.concatenate([hi, lo], axis=0)


def _router_bias_rows(router_b):
    b = jnp.transpose(router_b.astype(F32).reshape(N_GROUPS, GROUP_SIZE), (1, 0))
    return jnp.pad(b, ((0, 0), (0, 8 - N_GROUPS))).reshape(32, 1)


def kernel(x, mem, w_in_a, sgu_ln_g, sgu_ln_b, sgu_w, sgu_b, w_in_b, conv_w, w_kv, w_out, ln1_g, ln1_b,
           router_w, router_b, w_gate, w_up, w_down, ln2_g, ln2_b):
    cfg = _cfg()
    kt, v = _kv_call(mem, w_kv.astype(BF16))
    p = {
        "w_in_a": w_in_a.astype(BF16),
        "sgu_ln_g": sgu_ln_g[:, None, :],
        "sgu_ln_b": sgu_ln_b[:, None, :],
        "sgu_w": sgu_w,
        "sgu_b_col": jnp.broadcast_to(sgu_b[..., None], sgu_b.shape + (SGU_HEAD_DIM,)),
        "w_in_b": w_in_b.astype(BF16),
        "conv_w": conv_w,
        "kt": kt,
        "v": v,
        "w_out": w_out.astype(BF16),
        "ln1_g": ln1_g[:, None, :],
        "ln1_b": ln1_b[:, None, :],
        "rw": _router_rows(router_w),
        "rb": _router_bias_rows(router_b),
        "w_gu": jnp.concatenate([w_gate, w_up], axis=-1).astype(BF16),
        "w_down": w_down.astype(BF16),
        "ln2_g": ln2_g[:, None, :],
        "ln2_b": ln2_b[:, None, :],
    }
    h = x.reshape(cfg.n_tokens, D_MODEL)
    for layer in range(DEPTH):
        mixer = "gmlp" if layer % 2 == 0 else "conv"
        xs, cnt, cur, nxt, tbl = _mixer_call(mixer, h, p, layer, layer // 2)
        h = _moe_call(_moe_plan(cnt, cur, nxt, tbl), xs, p, layer)
    return h.reshape(BATCH, SEQ, D_MODEL)
```

```python
import types

import jax
import jax.numpy as jnp
from jax import lax
from jax.experimental import pallas as pl
from jax.experimental.pallas import tpu as pltpu

D_MODEL = 1024
BATCH = 16
SEQ = 4096
DEPTH = 2
CHUNK = 64
MIX_WIDTH = D_MODEL // 2
SGU_CHUNK = 128
SGU_HEADS = 4
SGU_HEAD_DIM = MIX_WIDTH // SGU_HEADS
CONV_WIDTH = 3
N_MEM = 256
XA_HEADS = 4
XA_HEAD_DIM = MIX_WIDTH // XA_HEADS
N_EXPERTS = 16
N_GROUPS = 4
GROUP_SIZE = N_EXPERTS // N_GROUPS
D_EXPERT = D_MODEL // 2
ALPHA = (2 * DEPTH) ** 0.25
LN_EPS = 1e-5

N_PAIRS = GROUP_SIZE * (GROUP_SIZE - 1) // 2
N_CLASSES = N_GROUPS * N_PAIRS
CLASS_ROWS = 32
PAIR_LO = (0, 0, 0, 1, 1, 2)
PAIR_HI = (1, 2, 3, 2, 3, 3)

LANES = 128
TM = 512
MOE_TILE = 256
ROW_W = D_MODEL + LANES
META_IDX, META_GA, META_GB = 0, 1, 2
HALO = 8
MIXER_STAGES = 4
N_STAGES = 3
MIXER_SLAB = 256
MIXER_SPLIT = 2
MIXER_ISSUE_ROWS = 16
MOE_SLAB = 256
VMEM_LIMIT = 56 * 1024 * 1024
KIND_FULL, KIND_PARTIAL, KIND_IDLE = 0, 1, 2

F32 = jnp.float32
BF16 = jnp.bfloat16
I32 = jnp.int32


def _cfg():
    n_tokens = BATCH * SEQ
    n_full_max = n_tokens // MOE_TILE
    n_pool_tiles = n_full_max + N_CLASSES
    assert n_pool_tiles <= TM and SEQ % TM == 0 and TM % SGU_CHUNK == 0
    assert n_tokens // TM >= MIXER_STAGES and n_tokens - N_CLASSES * (MOE_TILE - 1) >= N_STAGES * MOE_TILE
    return types.SimpleNamespace(
        n_tokens=n_tokens, n_mixer_steps=n_tokens // TM, tiles_per_seq=SEQ // TM,
        n_full_max=n_full_max, n_pool_tiles=n_pool_tiles, n_pool_rows=n_pool_tiles * MOE_TILE,
        n_moe_steps=n_full_max + 1 + N_CLASSES)


def _dot(a, b):
    return jnp.dot(a, b, preferred_element_type=F32)


def _dot_nt(a, b):
    return lax.dot_general(a, b, (((1,), (1,)), ((), ())), preferred_element_type=F32)


def _layer_norm(x, g, b):
    mu = jnp.mean(x, axis=-1, keepdims=True)
    xc = x - mu
    var = jnp.mean(xc * xc, axis=-1, keepdims=True)
    return xc * lax.rsqrt(var + LN_EPS) * g + b


def _sigmoid(x):
    return 1.0 / (1.0 + jnp.exp(-x))


def _row_copy(src_ref, src_row, dst_ref, dst_row, sem):
    return pltpu.make_async_copy(src_ref.at[pl.ds(src_row, 1)], dst_ref.at[pl.ds(dst_row, 1)], sem)


def _kv_kernel(mem_ref, wkv_ref, kt_ref, v_ref):
    kv = _dot(mem_ref[0].astype(BF16), wkv_ref[0])
    kt_ref[0, 0] = kv[:, :MIX_WIDTH].T.astype(BF16)
    v_ref[0, 0] = kv[:, MIX_WIDTH:].astype(BF16)


def _kv_call(mem, wkv_bf16):
    return pl.pallas_call(
        _kv_kernel,
        grid=(DEPTH, BATCH),
        in_specs=[
            pl.BlockSpec((1, N_MEM, D_MODEL), lambda l, b: (b, 0, 0)),
            pl.BlockSpec((1, D_MODEL, 2 * MIX_WIDTH), lambda l, b: (l, 0, 0)),
        ],
        out_specs=[
            pl.BlockSpec((1, 1, MIX_WIDTH, N_MEM), lambda l, b: (l, b, 0, 0)),
            pl.BlockSpec((1, 1, N_MEM, MIX_WIDTH), lambda l, b: (l, b, 0, 0)),
        ],
        out_shape=[
            jax.ShapeDtypeStruct((DEPTH, BATCH, MIX_WIDTH, N_MEM), BF16),
            jax.ShapeDtypeStruct((DEPTH, BATCH, N_MEM, MIX_WIDTH), BF16),
        ],
        name="kv_proj",
    )(mem, wkv_bf16)


def _route(y, step, rw_ref, rb_ref, tri_ref, low_ref, cnt_ref, cur_ref, nxt_ref, tbl_ref):
    y_hi = y.astype(BF16)
    y_lo = (y - y_hi.astype(F32)).astype(BF16)
    l_hi = _dot_nt(rw_ref[...], y_hi)
    l_lo = _dot_nt(rw_ref[0:32, :], y_lo)
    logits = l_hi[0:32] + l_hi[32:64] + l_lo
    scores = _sigmoid(logits)
    sel = scores + rb_ref[...]

    row = lax.broadcasted_iota(I32, (8, TM), 0).astype(F32)
    s = [sel[8 * m:8 * m + 8] for m in range(GROUP_SIZE)]
    sc = [scores[8 * m:8 * m + 8] for m in range(GROUP_SIZE)]
    neg = jnp.float32(-jnp.inf)
    m1 = jnp.maximum(jnp.maximum(s[0], s[1]), jnp.maximum(s[2], s[3]))
    i1 = jnp.where(s[0] == m1, 0.0, jnp.where(s[1] == m1, 1.0, jnp.where(s[2] == m1, 2.0, 3.0)))
    t = [jnp.where(i1 == float(m), neg, s[m]) for m in range(GROUP_SIZE)]
    m2 = jnp.maximum(jnp.maximum(t[0], t[1]), jnp.maximum(t[2], t[3]))
    i2 = jnp.where(t[0] == m2, 0.0, jnp.where(t[1] == m2, 1.0, jnp.where(t[2] == m2, 2.0, 3.0)))
    gscore = jnp.where(row < float(N_GROUPS), m1 + m2, neg)
    gmax = jnp.max(gscore, axis=0, keepdims=True)
    top_group = jnp.min(jnp.where(gscore == gmax, row, 8.0), axis=0, keepdims=True)
    lo = jnp.minimum(i1, i2)
    hi = jnp.maximum(i1, i2)
    pair = jnp.where(lo == 0.0, 0.0, jnp.where(lo == 1.0, 3.0, 5.0)) + hi - lo - 1.0
    s_lo = sum(jnp.where(lo == float(m), sc[m], 0.0) for m in range(GROUP_SIZE))
    s_hi = sum(jnp.where(hi == float(m), sc[m], 0.0) for m in range(GROUP_SIZE))
    inv = 1.0 / (s_lo + s_hi)
    chosen = row == top_group
    cls = jnp.sum(jnp.where(chosen, row * float(N_PAIRS) + pair, 0.0), axis=0, keepdims=True)
    ga = jnp.sum(jnp.where(chosen, s_lo * inv, 0.0), axis=0, keepdims=True)
    gb = jnp.sum(jnp.where(chosen, s_hi * inv, 0.0), axis=0, keepdims=True)

    crow = lax.broadcasted_iota(I32, (CLASS_ROWS, TM), 0).astype(F32)
    onehot = jnp.where(crow == cls, 1.0, 0.0)
    before = _dot(onehot.astype(BF16), tri_ref[...])
    cnt = cnt_ref[...]
    cur = cur_ref[...]
    nxt = nxt_ref[0:1, :]
    tile_f = float(MOE_TILE)
    inv_tile = 1.0 / MOE_TILE
    rank = cnt + before
    cnt_new = cnt + jnp.sum(onehot, axis=1, keepdims=True)
    tiles_old = jnp.floor((cnt + (tile_f - 1.0)) * inv_tile)
    tiles_new = jnp.floor((cnt_new + (tile_f - 1.0)) * inv_tile)
    fresh = tiles_new - tiles_old
    base = nxt + _dot(low_ref[...], fresh.astype(BF16))
    q = jnp.floor(rank * inv_tile)
    tile_of = jnp.where(q < tiles_old, cur, base + (q - tiles_old))
    pos_all = tile_of * tile_f + (rank - q * tile_f)
    pos = jnp.sum(onehot * pos_all, axis=0, keepdims=True)

    cnt_ref[...] = cnt_new
    cur_ref[...] = jnp.where(fresh > 0.0, base + fresh - 1.0, cur)
    nxt_ref[...] = jnp.broadcast_to(nxt + jnp.sum(fresh, axis=0, keepdims=True), nxt_ref.shape)
    ids = lax.broadcasted_iota(I32, (CLASS_ROWS, TM), 1).astype(F32)
    hit = jnp.where((ids >= base) & (ids < base + fresh), 1.0, 0.0)
    cls_hit = jnp.sum(hit * crow, axis=0, keepdims=True)
    any_hit = jnp.sum(hit, axis=0, keepdims=True)
    tbl_ref[...] = jnp.where(any_hit > 0.0, cls_hit, tbl_ref[...])

    idx = (lax.broadcasted_iota(I32, (1, TM), 1) + step * TM).astype(F32)
    return pos, idx, ga, gb


def _make_mixer_kernel(mixer, cfg):
    def body(*refs):
        if mixer == "gmlp":
            (x_ref, win_ref, lng_ref, lnb_ref, sw_ref, sb_ref, *rest) = refs
        else:
            (x_ref, xh_ref, win_ref, cw_ref, *rest) = refs
        (kt_ref, v_ref, wout_ref, g1_ref, b1_ref, rw_ref, rb_ref,
         xs_ref, cnt_out, cur_out, nxt_out, tbl_out,
         tri_ref, low_ref, cnt_ref, cur_ref, nxt_ref, tbl_ref, buf_ref, posv_ref, poss_ref,
         stv_ref, sts_ref, scat_sem, pos_sem, st_sem) = rest

        step = pl.program_id(0)
        last = pl.num_programs(0) - 1
        slot = step % MIXER_STAGES
        out_stage = (step + MIXER_STAGES - 2) % MIXER_STAGES
        pbuf = step % 2
        zbuf = buf_ref.at[pl.ds(MIXER_STAGES * TM, TM)]
        catbuf = buf_ref.at[pl.ds((MIXER_STAGES + 1) * TM, TM), pl.ds(0, D_MODEL)]
        obuf = buf_ref.at[pl.ds((MIXER_STAGES + 1) * TM, TM), pl.ds(D_MODEL, D_MODEL)]

        def stage_copy(stage, r, b):
            tile = buf_ref.at[pl.ds(pl.multiple_of(stage * TM, TM), TM), pl.ds(0, ROW_W)]
            return pltpu.make_async_copy(tile.at[pl.ds(r, 1)], xs_ref.at[pl.ds(poss_ref[b, r], 1)],
                                         scat_sem.at[stage])

        def pos_copy(b):
            return pltpu.make_async_copy(posv_ref.at[pl.ds(b, 1)], poss_ref.at[pl.ds(b, 1)], pos_sem.at[b])

        def stage_wait(stage):
            pltpu.make_async_copy(buf_ref.at[pl.ds(0, TM), pl.ds(0, ROW_W)], xs_ref.at[pl.ds(0, TM)],
                                  scat_sem.at[stage]).wait()

        @pl.when(step == 0)
        def _():
            r = lax.broadcasted_iota(I32, (TM, TM), 0)
            c = lax.broadcasted_iota(I32, (TM, TM), 1)
            tri_ref[...] = jnp.where(r < c, 1.0, 0.0).astype(BF16)
            r = lax.broadcasted_iota(I32, (CLASS_ROWS, CLASS_ROWS), 0)
            c = lax.broadcasted_iota(I32, (CLASS_ROWS, CLASS_ROWS), 1)
            low_ref[...] = jnp.where(c < r, 1.0, 0.0).astype(BF16)
            cnt_ref[...] = jnp.zeros_like(cnt_ref)
            cur_ref[...] = jnp.full_like(cur_ref, -1.0)
            nxt_ref[...] = jnp.zeros_like(nxt_ref)
            tbl_ref[...] = jnp.full_like(tbl_ref, -1.0)
            buf_ref[pl.ds((MIXER_STAGES - 2) * TM, 2 * TM), 0:ROW_W] = jnp.zeros((2 * TM, ROW_W), F32)

            def init_pos(r, carry):
                poss_ref[0, r] = r
                poss_ref[1, r] = r
                return carry

            lax.fori_loop(0, TM, init_pos, 0)

        @pl.when(step >= 2)
        def _():
            pos_copy(pbuf).wait()

        issued = [0]

        def issue_next():
            lo = issued[0]
            hi = min(lo + MIXER_ISSUE_ROWS, TM)
            for r in range(lo, hi):
                stage_copy(out_stage, r, pbuf).start(priority=r % 2)
            issued[0] = hi

        hrows = TM // MIXER_SPLIT
        n_in_slabs = win_ref.shape[-1] // MIXER_SLAB
        q_slabs = MIX_WIDTH // MIXER_SLAB
        q_lo = win_ref.shape[-1] - MIX_WIDTH
        scale = XA_HEAD_DIM ** -0.5
        xg, conv_tail = {}, {}

        def in_proj(g):
            rows = slice(g * hrows, (g + 1) * hrows)
            xg[g] = x_ref[rows, :]
            xb = xg[g].astype(BF16)
            for n in list(range(n_in_slabs - q_slabs, n_in_slabs)) + list(range(n_in_slabs - q_slabs)):
                cols = slice(n * MIXER_SLAB, (n + 1) * MIXER_SLAB)
                issue_next()
                zbuf[rows, cols] = _dot(xb, win_ref[0, :, cols])

        def attention(g):
            rows = slice(g * hrows, (g + 1) * hrows)
            for h in range(XA_HEADS):
                lo, hi = h * XA_HEAD_DIM, (h + 1) * XA_HEAD_DIM
                qh = zbuf[rows, q_lo + lo:q_lo + hi].astype(BF16)
                sc = _dot(qh, kt_ref[0, 0, lo:hi, :]) * scale
                e = jnp.exp(sc - jnp.max(sc, axis=-1, keepdims=True))
                p = e * (1.0 / jnp.sum(e, axis=-1, keepdims=True))
                issue_next()
                catbuf[rows, MIX_WIDTH + lo:MIX_WIDTH + hi] = _dot(p.astype(BF16), v_ref[0, 0, :, lo:hi])

        def token_mixer(g):
            rows = slice(g * hrows, (g + 1) * hrows)
            if mixer == "gmlp":
                zuv = zbuf[rows, 0:2 * MIX_WIDTH]
                uv = zuv * (lax.erf(zuv * (2.0 ** -0.5)) + 1.0) * 0.5
                u = uv[:, :MIX_WIDTH]
                vn = _layer_norm(uv[:, MIX_WIDTH:], lng_ref[0], lnb_ref[0]).astype(BF16)
                pos_r = lax.broadcasted_iota(I32, (SGU_CHUNK, SGU_CHUNK), 0) // CHUNK
                pos_c = lax.broadcasted_iota(I32, (SGU_CHUNK, SGU_CHUNK), 1) // CHUNK
                causal = pos_r >= pos_c
                for c in range(hrows // SGU_CHUNK):
                    crows = slice(c * SGU_CHUNK, (c + 1) * SGU_CHUNK)
                    blocks = []
                    for h in range(SGU_HEADS):
                        w = jnp.where(causal, sw_ref[0, h], 0.0).astype(BF16)
                        blk = vn[crows, h * SGU_HEAD_DIM:(h + 1) * SGU_HEAD_DIM]
                        blocks.append(_dot(w, blk) + sb_ref[0, h])
                    issue_next()
                    dst = slice(rows.start + crows.start, rows.start + crows.stop)
                    catbuf[dst, 0:MIX_WIDTH] = u[crows] * jnp.concatenate(blocks, axis=1)
            else:
                if g == 0:
                    zh = _dot(xh_ref[...].astype(BF16), win_ref[0, :, MIX_WIDTH:3 * MIX_WIDTH])
                    first_of_seq = (step % cfg.tiles_per_seq) == 0
                    hprev = jnp.where(first_of_seq, 0.0, zh[:, :MIX_WIDTH] * zh[:, MIX_WIDTH:])
                else:
                    hprev = conv_tail[g - 1]
                bg = zbuf[rows, 0:MIX_WIDTH]
                hcur = zbuf[rows, MIX_WIDTH:2 * MIX_WIDTH] * zbuf[rows, 2 * MIX_WIDTH:3 * MIX_WIDTH]
                conv_tail[g] = hcur[hrows - HALO:hrows]
                rowid = lax.broadcasted_iota(I32, (hrows, MIX_WIDTH), 0)
                h1 = jnp.where(rowid == 0, hprev[HALO - 1:HALO], pltpu.roll(hcur, 1, 0))
                h2 = jnp.where(rowid == 0, hprev[HALO - 2:HALO - 1],
                               jnp.where(rowid == 1, hprev[HALO - 1:HALO], pltpu.roll(hcur, 2, 0)))
                cw = cw_ref[0]
                issue_next()
                catbuf[rows, 0:MIX_WIDTH] = bg * (cw[0:1] * h2 + cw[1:2] * h1 + cw[2:3] * hcur)

        def out_proj(g):
            rows = slice(g * hrows, (g + 1) * hrows)
            cat = catbuf[rows, 0:D_MODEL].astype(BF16)
            for n in range(D_MODEL // MIXER_SLAB):
                cols = slice(n * MIXER_SLAB, (n + 1) * MIXER_SLAB)
                issue_next()
                obuf[rows, cols] = _dot(cat, wout_ref[0, :, cols])

        assert MIXER_SPLIT == 2
        in_proj(0)
        in_proj(1)
        attention(0)
        token_mixer(0)
        attention(1)
        token_mixer(1)
        out_proj(0)
        out_proj(1)
        while issued[0] < TM:
            issue_next()
        y = jnp.concatenate(
            [_layer_norm(ALPHA * xg[g] + obuf[g * hrows:(g + 1) * hrows, :], g1_ref[0], b1_ref[0])
             for g in range(MIXER_SPLIT)], axis=0)

        pos, idx, ga, gb = _route(y, step, rw_ref, rb_ref, tri_ref, low_ref, cnt_ref, cur_ref, nxt_ref, tbl_ref)
        mrow = lax.broadcasted_iota(I32, (LANES, TM), 0)
        meta = jnp.where(mrow == META_IDX, idx,
                         jnp.where(mrow == META_GA, ga, jnp.where(mrow == META_GB, gb, 0.0)))
        meta_t = jnp.transpose(meta)
        posv_ref[pl.ds(pbuf, 1), :] = pos.astype(I32)
        cnt_out[...] = cnt_ref[:, 0:LANES]
        cur_out[...] = cur_ref[:, 0:LANES]
        nxt_out[...] = nxt_ref[:, 0:LANES]
        tbl_out[...] = tbl_ref[...]

        @pl.when(step < 2)
        def _():
            stage_wait(out_stage)

        @pl.when(step >= MIXER_STAGES)
        def _():
            stage_wait(slot)

        own = pl.ds(pl.multiple_of(slot * TM, TM), TM)
        buf_ref[own, 0:D_MODEL] = y
        buf_ref[own, D_MODEL:ROW_W] = meta_t
        pos_copy(pbuf).start()

        @pl.when(step == last)
        def _():
            pos_copy(0).wait()
            pos_copy(1).wait()
            prev_stage = (step + MIXER_STAGES - 1) % MIXER_STAGES

            def issue_last(r, carry):
                stage_copy(prev_stage, r, 1 - pbuf).start()
                stage_copy(slot, r, pbuf).start()
                return carry

            lax.fori_loop(0, TM, issue_last, 0, unroll=8)
            lane = lax.broadcasted_iota(I32, (CLASS_ROWS, LANES), 1).astype(F32)
            crow = lax.broadcasted_iota(I32, (CLASS_ROWS, LANES), 0).astype(F32)
            diag = crow == lane
            cnt_l = jnp.sum(jnp.where(diag, cnt_ref[:, 0:LANES], 0.0), axis=0, keepdims=True)
            cur_l = jnp.sum(jnp.where(diag, cur_ref[:, 0:LANES], 0.0), axis=0, keepdims=True)
            srow = lax.broadcasted_iota(I32, (8, LANES), 0)
            st = jnp.where(srow == 0, cnt_l, jnp.where(srow == 1, cur_l, nxt_ref[:, 0:LANES]))
            stv_ref[...] = st.astype(I32)
            cp = pltpu.make_async_copy(stv_ref, sts_ref, st_sem)
            cp.start()
            cp.wait()
            for stage in range(MIXER_STAGES):
                stage_wait(stage)
            zero_row = pl.multiple_of(out_stage * TM, TM)
            buf_ref[pl.ds(zero_row, TM), 0:ROW_W] = jnp.zeros((TM, ROW_W), F32)
            zrow = buf_ref.at[pl.ds(zero_row, MOE_TILE), pl.ds(0, ROW_W)]
            zsem = scat_sem.at[0]

            def clear_class(c, carry):
                fill = sts_ref[0, c] % MOE_TILE
                start = sts_ref[1, c] * MOE_TILE + fill
                n = jnp.where(fill == 0, 0, MOE_TILE - fill)

                def one(j, carry2):
                    _row_copy(zrow, 0, xs_ref, start + j, zsem).start()
                    return carry2

                def one_wait(j, carry2):
                    _row_copy(zrow, 0, xs_ref, 0, zsem).wait()
                    return carry2

                lax.fori_loop(0, n, one, 0)
                lax.fori_loop(0, n, one_wait, 0)
                return carry

            lax.fori_loop(0, N_CLASSES, clear_class, 0)

            def clear_tile(t, carry):
                start = pl.multiple_of(t * MOE_TILE, MOE_TILE)
                cp2 = pltpu.make_async_copy(zrow, xs_ref.at[pl.ds(start, MOE_TILE)], zsem)
                cp2.start()
                cp2.wait()
                return carry

            lax.fori_loop(sts_ref[2, 0], cfg.n_pool_tiles, clear_tile, 0)

    return body


def _mixer_call(mixer, x, p, layer, j):
    cfg = _cfg()

    def const_spec(shape):
        return pl.BlockSpec(shape, lambda i: (0,) * len(shape))

    def layer_spec(shape, l):
        return pl.BlockSpec((1,) + shape, lambda i: (l,) + (0,) * len(shape))

    x_spec = pl.BlockSpec((TM, D_MODEL), lambda i: (i, 0))
    if mixer == "gmlp":
        head_specs = [
            x_spec,
            layer_spec((D_MODEL, 3 * MIX_WIDTH), j),
            layer_spec((1, MIX_WIDTH), j),
            layer_spec((1, MIX_WIDTH), j),
            layer_spec((SGU_HEADS, SGU_CHUNK, SGU_CHUNK), j),
            layer_spec((SGU_HEADS, SGU_CHUNK, SGU_HEAD_DIM), j),
        ]
        head_args = (x, p["w_in_a"], p["sgu_ln_g"], p["sgu_ln_b"], p["sgu_w"], p["sgu_b_col"])
    else:
        head_specs = [
            x_spec,
            pl.BlockSpec((HALO, D_MODEL), lambda i: (jnp.maximum(i * (TM // HALO) - 1, 0), 0)),
            layer_spec((D_MODEL, 4 * MIX_WIDTH), j),
            layer_spec((CONV_WIDTH, MIX_WIDTH), j),
        ]
        head_args = (x, x, p["w_in_b"], p["conv_w"])
    tail_specs = [
        pl.BlockSpec((1, 1, MIX_WIDTH, N_MEM), lambda i: (layer, i // cfg.tiles_per_seq, 0, 0)),
        pl.BlockSpec((1, 1, N_MEM, MIX_WIDTH), lambda i: (layer, i // cfg.tiles_per_seq, 0, 0)),
        layer_spec((D_MODEL, D_MODEL), layer),
        layer_spec((1, D_MODEL), layer),
        layer_spec((1, D_MODEL), layer),
        const_spec((64, D_MODEL)),
        const_spec((32, 1)),
    ]
    tail_args = (p["kt"], p["v"], p["w_out"], p["ln1_g"], p["ln1_b"], p["rw"], p["rb"])
    return pl.pallas_call(
        _make_mixer_kernel(mixer, cfg),
        grid=(cfg.n_mixer_steps,),
        in_specs=head_specs + tail_specs,
        out_specs=[
            pl.BlockSpec(memory_space=pl.ANY),
            const_spec((CLASS_ROWS, LANES)),
            const_spec((CLASS_ROWS, LANES)),
            const_spec((8, LANES)),
            const_spec((8, TM)),
        ],
        out_shape=[
            jax.ShapeDtypeStruct((cfg.n_pool_rows, ROW_W), F32),
            jax.ShapeDtypeStruct((CLASS_ROWS, LANES), F32),
            jax.ShapeDtypeStruct((CLASS_ROWS, LANES), F32),
            jax.ShapeDtypeStruct((8, LANES), F32),
            jax.ShapeDtypeStruct((8, TM), F32),
        ],
        scratch_shapes=[
            pltpu.VMEM((TM, TM), BF16),
            pltpu.VMEM((CLASS_ROWS, CLASS_ROWS), BF16),
            pltpu.VMEM((CLASS_ROWS, TM), F32),
            pltpu.VMEM((CLASS_ROWS, TM), F32),
            pltpu.VMEM((8, TM), F32),
            pltpu.VMEM((8, TM), F32),
            pltpu.VMEM(((MIXER_STAGES + 2) * TM, 2 * D_MODEL), F32),
            pltpu.VMEM((2, TM), I32),
            pltpu.SMEM((2, TM), I32),
            pltpu.VMEM((8, LANES), I32),
            pltpu.SMEM((8, LANES), I32),
            pltpu.SemaphoreType.DMA((MIXER_STAGES,)),
            pltpu.SemaphoreType.DMA((2,)),
            pltpu.SemaphoreType.DMA(()),
        ],
        compiler_params=pltpu.CompilerParams(dimension_semantics=("arbitrary",), vmem_limit_bytes=VMEM_LIMIT),
        name="mixer_" + mixer,
    )(*head_args, *tail_args)


def _make_moe_kernel(cfg):
    n_slabs = D_MODEL // MOE_SLAB
    rows_per_chunk = MOE_TILE // (4 * n_slabs)
    work_row, facc_row = N_STAGES * MOE_TILE, (N_STAGES + 2) * MOE_TILE

    def body(blk_ref, ea_ref, eb_ref, kind_ref, nvalid_ref, nfull_ref,
             xs_ref, wgu_a_ref, wd_a_ref, wgu_b_ref, wd_b_ref, g2_ref, b2_ref, out_ref,
             buf_ref, idxv_ref, idxs_ref, scat_sem, idx_sem):
        step = pl.program_id(0)
        slot = step % N_STAGES
        prev = (step + N_STAGES - 1) % N_STAGES
        works = [buf_ref.at[pl.ds(work_row + k * MOE_TILE, MOE_TILE)] for k in range(2)]
        facc = buf_ref.at[pl.ds(facc_row, MOE_TILE)]

        ibuf = step % 2
        pbuf = 1 - ibuf

        def stage_copy(stage, r, b):
            tile = buf_ref.at[pl.ds(pl.multiple_of(stage * MOE_TILE, MOE_TILE), MOE_TILE)]
            return _row_copy(tile, r, out_ref, idxs_ref[b * 8, r], scat_sem.at[stage])

        def stage_wait(stage):
            pltpu.make_async_copy(buf_ref.at[pl.ds(0, MOE_TILE)], out_ref.at[pl.ds(0, MOE_TILE)],
                                  scat_sem.at[stage]).wait()

        def experts(issue):
            x = xs_ref[:, 0:D_MODEL]
            xb = x.astype(BF16)
            meta = xs_ref[:, D_MODEL:ROW_W]
            gates = (meta[:, META_GA:META_GA + 1], meta[:, META_GB:META_GB + 1])
            pairs = ((wgu_a_ref, wd_a_ref), (wgu_b_ref, wd_b_ref))
            k_slab = 0
            for k, (wgu_ref, _) in enumerate(pairs):
                for n in range(n_slabs):
                    cols = slice(n * MOE_SLAB, (n + 1) * MOE_SLAB)
                    issue(k_slab)
                    k_slab += 1
                    works[k][:, cols] = _dot(xb, wgu_ref[0, 0, :, cols])
            for k, (_, wd_ref) in enumerate(pairs):
                g = works[k][:, :D_EXPERT]
                hb = (g * _sigmoid(g) * works[k][:, D_EXPERT:]).astype(BF16)
                for n in range(n_slabs):
                    cols = slice(n * MOE_SLAB, (n + 1) * MOE_SLAB)
                    issue(k_slab)
                    k_slab += 1
                    part = gates[k] * _dot(hb, wd_ref[0, 0, :, cols])
                    facc[:, cols] = part if k == 0 else facc[:, cols] + part
            return _layer_norm(ALPHA * x + facc[...], g2_ref[0], b2_ref[0])

        def idx_copy(b):
            dst = idxs_ref.at[pl.ds(pl.multiple_of(b * 8, 8), 8)]
            return pltpu.make_async_copy(idxv_ref, dst, idx_sem.at[b])

        def put_idx(b):
            idx = jnp.transpose(xs_ref[:, D_MODEL:ROW_W])[META_IDX:META_IDX + 1, :]
            idxv_ref[...] = jnp.broadcast_to(idx, idxv_ref.shape).astype(I32)
            return idx_copy(b)

        @pl.when(step == 0)
        def _():
            buf_ref[pl.ds((N_STAGES - 1) * MOE_TILE, MOE_TILE), :] = jnp.zeros((MOE_TILE, D_MODEL), F32)

            def init_idx(r, carry):
                idxs_ref[8, r] = r
                return carry

            lax.fori_loop(0, MOE_TILE, init_idx, 0)

        @pl.when(kind_ref[step] == KIND_FULL)
        def _():
            @pl.when(step > 0)
            def _():
                idx_copy(pbuf).wait()

            @pl.when(step < nfull_ref[0])
            def _():
                put_idx(ibuf).start()

            def issue(chunk):
                for r in range(chunk * rows_per_chunk, (chunk + 1) * rows_per_chunk):
                    stage_copy(prev, r, pbuf).start(priority=r % 2)

            y = experts(issue)

            @pl.when(step == 0)
            def _():
                stage_wait(prev)

            @pl.when(step >= N_STAGES)
            def _():
                stage_wait(slot)

            buf_ref[pl.ds(pl.multiple_of(slot * MOE_TILE, MOE_TILE), MOE_TILE), :] = y

            @pl.when(step == nfull_ref[0])
            def _():
                stage_wait((step + 1) % N_STAGES)
                stage_wait(prev)

        @pl.when(kind_ref[step] == KIND_PARTIAL)
        def _():
            cp = put_idx(0)
            cp.start()
            buf_ref[pl.ds(0, MOE_TILE), :] = experts(lambda chunk: None)
            cp.wait()
            n = nvalid_ref[step]

            def one(r, carry):
                stage_copy(0, r, 0).start()
                return carry

            def one_wait(r, carry):
                _row_copy(buf_ref, 0, out_ref, 0, scat_sem.at[0]).wait()
                return carry

            lax.fori_loop(0, n, one, 0)
            lax.fori_loop(0, n, one_wait, 0)

    return body


def _moe_call(plan, xs, p, layer):
    cfg = _cfg()

    def xmap(i, blk, *_):
        return (blk[i], 0)

    def wa(i, blk, ea, *_):
        return (layer, ea[i], 0, 0)

    def wb(i, blk, ea, eb, *_):
        return (layer, eb[i], 0, 0)

    def lmap(i, *_):
        return (layer, 0, 0)

    return pl.pallas_call(
        _make_moe_kernel(cfg),
        grid_spec=pltpu.PrefetchScalarGridSpec(
            num_scalar_prefetch=6,
            grid=(cfg.n_moe_steps,),
            in_specs=[
                pl.BlockSpec((MOE_TILE, ROW_W), xmap),
                pl.BlockSpec((1, 1, D_MODEL, 2 * D_EXPERT), wa),
                pl.BlockSpec((1, 1, D_EXPERT, D_MODEL), wa),
                pl.BlockSpec((1, 1, D_MODEL, 2 * D_EXPERT), wb),
                pl.BlockSpec((1, 1, D_EXPERT, D_MODEL), wb),
                pl.BlockSpec((1, 1, D_MODEL), lmap),
                pl.BlockSpec((1, 1, D_MODEL), lmap),
            ],
            out_specs=pl.BlockSpec(memory_space=pl.ANY),
            scratch_shapes=[
                pltpu.VMEM(((N_STAGES + 3) * MOE_TILE, D_MODEL), F32),
                pltpu.VMEM((8, MOE_TILE), I32),
                pltpu.SMEM((16, MOE_TILE), I32),
                pltpu.SemaphoreType.DMA((N_STAGES,)),
                pltpu.SemaphoreType.DMA((2,)),
            ],
        ),
        out_shape=jax.ShapeDtypeStruct((cfg.n_tokens, D_MODEL), F32),
        compiler_params=pltpu.CompilerParams(dimension_semantics=("arbitrary",), vmem_limit_bytes=VMEM_LIMIT),
        name="moe_experts",
    )(*plan, xs, p["w_gu"], p["w_down"], p["w_gu"], p["w_down"], p["ln2_g"], p["ln2_b"])


def _moe_plan(cnt_out, cur_out, nxt_out, tbl_out):
    cfg = _cfg()
    cnt = cnt_out[:N_CLASSES, 0].astype(I32)
    cur = cur_out[:N_CLASSES, 0].astype(I32)
    n_used = nxt_out[0, 0].astype(I32)
    tile_cls = jnp.clip(tbl_out[0, :cfg.n_pool_tiles].astype(I32), 0, N_CLASSES - 1)
    tid = jnp.arange(cfg.n_pool_tiles, dtype=I32)
    fill = cnt % MOE_TILE
    ends_here = (cur[None, :] == tid[:, None]) & (fill[None, :] != 0)
    is_partial = jnp.any(ends_here, axis=1)
    n_valid = jnp.where(is_partial, jnp.sum(jnp.where(ends_here, fill[None, :], 0), axis=1), MOE_TILE)
    used = tid < n_used
    big = jnp.int32(1 << 30)
    key = tile_cls * (2 * cfg.n_pool_tiles) + tid
    full = used & ~is_partial
    part = used & is_partial

    def order_of(keys):
        rank = jnp.sum(keys[None, :] < keys[:, None], axis=1)
        return jnp.sum(jnp.where(rank[None, :] == tid[:, None], tid[None, :], 0), axis=1).astype(I32)

    order_full = order_of(jnp.where(full, key, big + tid))
    order_part = order_of(jnp.where(part, key, big + tid))
    n_full = jnp.sum(full).astype(I32)
    n_part = jnp.sum(part).astype(I32)
    last_full = order_full[n_full - 1]

    i = jnp.arange(cfg.n_full_max + 1, dtype=I32)
    blk_a = order_full[jnp.minimum(i, n_full - 1)]
    kind_a = jnp.where(i <= n_full, KIND_FULL, KIND_IDLE)
    k = jnp.arange(N_CLASSES, dtype=I32)
    blk_b = jnp.where(n_part > 0, order_part[jnp.clip(jnp.minimum(k, n_part - 1), 0, None)], last_full)
    kind_b = jnp.where(k < n_part, KIND_PARTIAL, KIND_IDLE)
    blk = jnp.concatenate([blk_a, blk_b]).astype(I32)
    kind = jnp.concatenate([kind_a, kind_b]).astype(I32)
    cls = tile_cls[blk]
    base = (cls // N_PAIRS) * GROUP_SIZE
    ea = base + jnp.asarray(PAIR_LO, I32)[cls % N_PAIRS]
    eb = base + jnp.asarray(PAIR_HI, I32)[cls % N_PAIRS]
    return blk, ea.astype(I32), eb.astype(I32), kind, n_valid[blk].astype(I32), n_full.reshape(1)


def _router_rows(router_w):
    wt = router_w.astype(F32).T.reshape(N_GROUPS, GROUP_SIZE, D_MODEL)
    wt = jnp.transpose(wt, (1, 0, 2))
    wt = jnp.pad(wt, ((0, 0), (0, 8 - N_GROUPS), (0, 0))).reshape(32, D_MODEL)
    hi = wt.astype(BF16)
    lo = (wt - hi.astype(F32)).astype(BF16)
    return jnp.concatenate([hi, lo], axis=0)


def _router_bias_rows(router_b):
    b = jnp.transpose(router_b.astype(F32).reshape(N_GROUPS, GROUP_SIZE), (1, 0))
    return jnp.pad(b, ((0, 0), (0, 8 - N_GROUPS))).reshape(32, 1)


def kernel(x, mem, w_in_a, sgu_ln_g, sgu_ln_b, sgu_w, sgu_b, w_in_b, conv_w, w_kv, w_out, ln1_g, ln1_b,
           router_w, router_b, w_gate, w_up, w_down, ln2_g, ln2_b):
    cfg = _cfg()
    kt, v = _kv_call(mem, w_kv.astype(BF16))
    p = {
        "w_in_a": w_in_a.astype(BF16),
        "sgu_ln_g": sgu_ln_g[:, None, :],
        "sgu_ln_b": sgu_ln_b[:, None, :],
        "sgu_w": sgu_w,
        "sgu_b_col": jnp.broadcast_to(sgu_b[..., None], sgu_b.shape + (SGU_HEAD_DIM,)),
        "w_in_b": w_in_b.astype(BF16),
        "conv_w": conv_w,
        "kt": kt,
        "v": v,
        "w_out": w_out.astype(BF16),
        "ln1_g": ln1_g[:, None, :],
        "ln1_b": ln1_b[:, None, :],
        "rw": _router_rows(router_w),
        "rb": _router_bias_rows(router_b),
        "w_gu": jnp.concatenate([w_gate, w_up], axis=-1).astype(BF16),
        "w_down": w_down.astype(BF16),
        "ln2_g": ln2_g[:, None, :],
        "ln2_b": ln2_b[:, None, :],
    }
    h = x.reshape(cfg.n_tokens, D_MODEL)
    for layer in range(DEPTH):
        mixer = "gmlp" if layer % 2 == 0 else "conv"
        xs, cnt, cur, nxt, tbl = _mixer_call(mixer, h, p, layer, layer // 2)
        h = _moe_call(_moe_plan(cnt, cur, nxt, tbl), xs, p, layer)
    return h.reshape(BATCH, SEQ, D_MODEL)
```
